```python
import jax, jax.numpy as jnp
from jax import lax
import numpy as np

D_MODEL = 1024
BATCH = 32
SEQ = 256
DEPTH = 1
DEC_BATCH = 2
DEC_SEQ = 2048
PAST_LEN = 512

GRID_W = 64
HGRN_HEADS = 8
HGRN_DK = 128
HGRN_DV = 128
HGRN_WIDTH = HGRN_HEADS * HGRN_DK
HGRN_CHUNK = 16
CONV_WIDTH = D_MODEL
CONV_K = 31
PEER_HEADS = 8
PEER_NKEYS = 128
PEER_N = PEER_NKEYS * PEER_NKEYS
PEER_QDIM = 256
PEER_HALF = PEER_QDIM // 2
PEER_TOPK = 16
PEER_TOKEN_BLOCK = 128
IN_WIDTH = 5 * HGRN_WIDTH + 2 * CONV_WIDTH + 2 * D_MODEL
EPS = 1e-6

kernel_name = "hybrid_hgrn2_conformer_peer_dit_step"


def rmsnorm(x, g):
    xf = x.astype(jnp.float32)
    y = xf * lax.rsqrt(jnp.mean(xf * xf, axis=-1, keepdims=True) + EPS)
    return y.astype(x.dtype) * g


def layernorm(x, g, b):
    xf = x.astype(jnp.float32)
    mu = jnp.mean(xf, axis=-1, keepdims=True)
    var = jnp.mean(jnp.square(xf - mu), axis=-1, keepdims=True)
    y = (xf - mu) * lax.rsqrt(var + EPS)
    return y.astype(x.dtype) * g + b


def chunk_scan(q, k, g, v, s0):
    B, H, L, dk = q.shape
    dv = v.shape[-1]
    n = L // HGRN_CHUNK

    def chunks(t):
        return t.reshape(B, H, n, HGRN_CHUNK, t.shape[-1]).transpose(2, 0, 1, 3, 4)

    causal = jnp.tril(jnp.ones((HGRN_CHUNK, HGRN_CHUNK), dtype=bool))

    def step(S, xs):
        qc, kc, gc, vc = xs
        bc = jnp.cumsum(gc, axis=-2)
        o_inter = jnp.einsum('bhtd,bhde->bhte', qc * jnp.exp(bc), S)
        diff = bc[..., :, None, :] - bc[..., None, :, :]
        decay = jnp.exp(jnp.where(causal[:, :, None], diff, -jnp.inf))
        att = jnp.einsum('bhtd,bhsd,bhtsd->bhts', qc, kc, decay)
        o_intra = jnp.einsum('bhts,bhse->bhte', att, vc)
        b_last = bc[..., -1:, :]
        S_new = jnp.exp(b_last[..., 0, :])[..., None] * S + jnp.einsum(
            'bhsd,bhse->bhde', kc * jnp.exp(b_last - bc), vc)
        return S_new, o_inter + o_intra

    s_fin, o = lax.scan(step, s0, (chunks(q), chunks(k), chunks(g), chunks(v)))
    o = o.transpose(1, 2, 0, 3, 4).reshape(B, H, L, dv)
    return o, s_fin


def hgrn2_bidir(q, f_fwd, f_bwd, iv, og, lb, s0, gnorm_g):
    B, L, _ = q.shape

    def heads(t):
        return t.reshape(B, L, HGRN_HEADS, -1).transpose(0, 2, 1, 3).astype(jnp.float32)

    qh = heads(jax.nn.silu(q))
    vh = heads(iv)
    s0f = s0.astype(jnp.float32)
    outs = []
    finals = []
    for d, fz in enumerate((f_fwd, f_bwd)):
        lbd = lb[d]
        z = fz.astype(jnp.float32)
        log_f = jnp.logaddexp(jnp.log(lbd), jnp.log1p(-lbd) + jax.nn.log_sigmoid(z))
        kk = (1.0 - lbd) * jax.nn.sigmoid(-z)
        qd, kd, gd, vd = qh, heads(kk), heads(log_f), vh
        if d == 1:
            qd, kd, gd, vd = (jnp.flip(t, axis=2) for t in (qd, kd, gd, vd))
        o, sf = chunk_scan(qd, kd, gd, vd, s0f[:, d])
        if d == 1:
            o = jnp.flip(o, axis=2)
        outs.append(o)
        finals.append(sf)
    o = rmsnorm(outs[0] + outs[1], gnorm_g)
    o = o.transpose(0, 2, 1, 3).reshape(B, L, -1).astype(og.dtype) * jax.nn.silu(og)
    return o, jnp.stack(finals, axis=1).astype(s0.dtype)


def depthwise_conv(u, w, b):
    out = lax.conv_general_dilated(
        u, w[:, None, :].astype(u.dtype), window_strides=(1,),
        padding=[(CONV_K // 2, CONV_K // 2)],
        dimension_numbers=('NWC', 'WIO', 'NWC'),
        feature_group_count=u.shape[-1])
    return out + b


def peer(h, wq, keys, u_tab, v_tab):
    B, L, D = h.shape
    T = B * L
    xt = h.reshape(T, D)
    qry = (xt @ wq).reshape(T, PEER_HEADS, 2, PEER_HALF)
    sc = jnp.einsum('thpd,hpnd->thpn', qry, keys).astype(jnp.float32)
    v1, i1 = lax.top_k(sc[:, :, 0], PEER_TOPK)
    v2, i2 = lax.top_k(sc[:, :, 1], PEER_TOPK)
    cand = (v1[..., :, None] + v2[..., None, :]).reshape(T, PEER_HEADS, PEER_TOPK * PEER_TOPK)
    best, pos = lax.top_k(cand, PEER_TOPK)
    ei = jnp.take_along_axis(i1, pos // PEER_TOPK, axis=-1)
    ej = jnp.take_along_axis(i2, pos % PEER_TOPK, axis=-1)
    expert = ei * PEER_NKEYS + ej
    gate = jax.nn.softmax(best, axis=-1).astype(h.dtype)
    nb = T // PEER_TOKEN_BLOCK

    def block(args):
        xb, eb, gb = args
        act = jax.nn.gelu(jnp.einsum('td,ted->te', xb, u_tab[eb]), approximate=False)
        return jnp.einsum('te,ted->td', gb * act, v_tab[eb])

    out = lax.map(block, (xt.reshape(nb, PEER_TOKEN_BLOCK, D),
                          expert.reshape(nb, PEER_TOKEN_BLOCK, -1),
                          gate.reshape(nb, PEER_TOKEN_BLOCK, -1)))
    return out.reshape(B, L, D)


def trunk_layer(x, cond, s0, lb, norm_g, w_ada, b_ada, w_in, hgrn_gnorm_g, w_hgrn_out,
                conv_w, conv_b, conv_ln_g, conv_ln_b, w_conv_out, w_mix_out,
                peer_wq, peer_keys, peer_u, peer_v):
    mod = jax.nn.silu(cond) @ w_ada + b_ada
    sh1, sc1, g1, sh2, sc2, g2 = jnp.split(mod[:, None, :], 6, axis=-1)
    h = rmsnorm(x, norm_g[0]) * (1.0 + sc1) + sh1
    proj = h @ w_in
    sizes = [HGRN_WIDTH] * 5 + [CONV_WIDTH] * 2 + [D_MODEL] * 2
    points = [int(p) for p in np.cumsum(sizes)[:-1]]
    q, f_f, f_b, iv, og, glu_a, glu_b, gt_a, gt_b = jnp.split(proj, points, axis=-1)
    y_a, s_fin = hgrn2_bidir(q, f_f, f_b, iv, og, lb, s0, hgrn_gnorm_g)
    y_a = y_a @ w_hgrn_out
    u = glu_a * jax.nn.sigmoid(glu_b)
    u = depthwise_conv(u, conv_w, conv_b)
    u = jax.nn.silu(layernorm(u, conv_ln_g, conv_ln_b))
    y_b = u @ w_conv_out
    mix = jax.nn.sigmoid(gt_a) * y_a + jax.nn.sigmoid(gt_b) * y_b
    x = x + g1 * (mix @ w_mix_out)
    h2 = rmsnorm(x, norm_g[1]) * (1.0 + sc2) + sh2
    x = x + g2 * peer(h2, peer_wq, peer_keys, peer_u, peer_v)
    return x, s_fin


def setup_inputs(seed: int = 0) -> dict:
    key = jax.random.key(seed)
    ks = jax.random.split(key, 24)

    def nrm(k, shape, scale):
        return jax.random.normal(k, shape, jnp.float32) * scale

    D = D_MODEL
    return {
        "x_prompt": nrm(ks[0], (BATCH, SEQ, D), 1.0),
        "x_sample": nrm(ks[1], (DEC_BATCH, DEC_SEQ, D), 1.0),
        "state_hgrn": nrm(ks[2], (DEC_BATCH, DEPTH, 2, HGRN_HEADS, HGRN_DK, HGRN_DV), 0.5),
        "c": nrm(ks[3], (DEC_BATCH, D), 1.0),
        "c_ctx": nrm(ks[4], (D,), 1.0),
        "norm_g": 1.0 + nrm(ks[5], (DEPTH, 2, D), 0.02),
        "w_ada": nrm(ks[6], (DEPTH, D, 6 * D), D ** -0.5),
        "b_ada": nrm(ks[7], (DEPTH, 6 * D), 0.02),
        "w_in": nrm(ks[8], (DEPTH, D, IN_WIDTH), D ** -0.5),
        "hgrn_lb_logits": nrm(ks[9], (DEPTH + 1, 2, HGRN_WIDTH), 0.5),
        "hgrn_gnorm_g": 1.0 + nrm(ks[10], (DEPTH, HGRN_DV), 0.02),
        "w_hgrn_out": nrm(ks[11], (DEPTH, HGRN_WIDTH, D), HGRN_WIDTH ** -0.5),
        "conv_w": nrm(ks[12], (DEPTH, CONV_K, CONV_WIDTH), CONV_K ** -0.5),
        "conv_b": nrm(ks[13], (DEPTH, CONV_WIDTH), 0.02),
        "conv_ln_g": 1.0 + nrm(ks[14], (DEPTH, CONV_WIDTH), 0.02),
        "conv_ln_b": nrm(ks[15], (DEPTH, CONV_WIDTH), 0.02),
        "w_conv_out": nrm(ks[16], (DEPTH, CONV_WIDTH, D), CONV_WIDTH ** -0.5),
        "w_mix_out": nrm(ks[17], (DEPTH, D, D), D ** -0.5),
        "peer_wq": nrm(ks[18], (DEPTH, D, PEER_HEADS * PEER_QDIM), D ** -0.5),
        "peer_keys": nrm(ks[19], (DEPTH, PEER_HEADS, 2, PEER_NKEYS, PEER_HALF), PEER_HALF ** -0.5),
        "peer_u": nrm(ks[20], (DEPTH, PEER_N, D), D ** -0.5),
        "peer_v": nrm(ks[21], (DEPTH, PEER_N, D), 0.5),
        "final_g": 1.0 + nrm(ks[22], (D,), 0.02),
    }


def reference(x_prompt, x_sample, state_hgrn, c, c_ctx, norm_g, w_ada, b_ada, w_in,
              hgrn_lb_logits, hgrn_gnorm_g, w_hgrn_out, conv_w, conv_b, conv_ln_g,
              conv_ln_b, w_conv_out, w_mix_out, peer_wq, peer_keys, peer_u, peer_v,
              final_g):
    n_prompt = x_prompt.shape[0]
    cond_ctx = jnp.broadcast_to(c_ctx, (n_prompt, D_MODEL))
    s_zero = jnp.zeros((n_prompt, 2, HGRN_HEADS, HGRN_DK, HGRN_DV), x_prompt.dtype)
    lb_all = jnp.cumsum(jax.nn.softmax(hgrn_lb_logits.astype(jnp.float32), axis=0), axis=0)
    xp = x_prompt
    xs = x_sample
    ctx_states = []
    for l in range(DEPTH):
        w = (norm_g[l], w_ada[l], b_ada[l], w_in[l], hgrn_gnorm_g[l], w_hgrn_out[l],
             conv_w[l], conv_b[l], conv_ln_g[l], conv_ln_b[l], w_conv_out[l], w_mix_out[l],
             peer_wq[l], peer_keys[l], peer_u[l], peer_v[l])
        xp, s_ctx = trunk_layer(xp, cond_ctx, s_zero, lb_all[l], *w)
        ctx_states.append(s_ctx)
        xs, _ = trunk_layer(xs, c, state_hgrn[:, l], lb_all[l], *w)
    y_prompt = rmsnorm(xp, final_g)
    y_sample = rmsnorm(xs, final_g)
    new_state_hgrn = jnp.stack(ctx_states, axis=1)
    return (y_prompt, y_sample, new_state_hgrn)
```

```python
import functools

import numpy as np
import jax
import jax.numpy as jnp
from jax import lax
from jax.experimental import pallas as pl
from jax.experimental.pallas import tpu as pltpu

D_MODEL = 1024
HEADS = 8
HEAD_DIM = 128
CONV_K = 31
CONV_HALO = 16
PEER_HEADS = 8
PEER_NKEYS = 128
PEER_TOPK = 16
EPS = 1e-6
CHUNK = 128
N_LEVELS = 7
MIX_TILE = 256
VMEM_LIMIT = 56 * 1024 * 1024

F32 = jnp.float32
BF16 = jnp.bfloat16


def _dot(a, b):
    return jnp.dot(a, b, preferred_element_type=F32)


def _dot_nt(a, b):
    return lax.dot_general(a, b, (((1,), (1,)), ((), ())), preferred_element_type=F32)


def _dot_tn(a, b):
    return lax.dot_general(a, b, (((0,), (0,)), ((), ())), preferred_element_type=F32)


def _split2(x):
    hi = x.astype(BF16)
    lo = (x - hi.astype(F32)).astype(BF16)
    return hi, lo


def _split3(x):
    hi = x.astype(BF16)
    r = x - hi.astype(F32)
    mid = r.astype(BF16)
    lo = (r - mid.astype(F32)).astype(BF16)
    return hi, mid, lo


def _sigmoid(x):
    return 1.0 / (1.0 + jnp.exp(-x))


def _rms_rows(x, g):
    ms = jnp.mean(x * x, axis=-1, keepdims=True)
    return x * lax.rsqrt(ms + EPS) * g


def _mod_row(tile, tile_tokens, n_prompt_tokens, sample_len):
    npt = n_prompt_tokens // tile_tokens
    tps = sample_len // tile_tokens
    return jnp.where(tile < npt, 0, 1 + (tile - npt) // tps)


def _ada_kernel(cond_ref, w_ref, b_ref, o_ref):
    c = cond_ref[...]
    s = c * _sigmoid(c)
    o_ref[...] = jnp.dot(s, w_ref[...], preferred_element_type=F32,
                         precision=lax.Precision.HIGHEST) + b_ref[...]


def _ada(cond8, w_ada, b_ada):
    n_out = w_ada.shape[1]
    bn = D_MODEL
    return pl.pallas_call(
        _ada_kernel,
        grid=(n_out // bn,),
        in_specs=[pl.BlockSpec((8, D_MODEL), lambda j: (0, 0)),
                  pl.BlockSpec((D_MODEL, bn), lambda j: (0, j)),
                  pl.BlockSpec((1, bn), lambda j: (0, j))],
        out_specs=pl.BlockSpec((8, bn), lambda j: (0, j)),
        out_shape=jax.ShapeDtypeStruct((8, n_out), F32),
        compiler_params=pltpu.CompilerParams(dimension_semantics=("parallel",),
                                             vmem_limit_bytes=VMEM_LIMIT),
        name="ada",
    )(cond8, w_ada, b_ada.reshape(1, n_out))


def _inproj_kernel(x_ref, mod_ref, ng_ref, w_ref, o_ref, h_scr):
    @pl.when(pl.program_id(1) == 0)
    def _():
        y = _rms_rows(x_ref[...], ng_ref[...])
        h = y * (1.0 + mod_ref[0, 1:2, :]) + mod_ref[0, 0:1, :]
        h_scr[...] = h.astype(BF16)

    o_ref[...] = _dot(h_scr[...], w_ref[...])


def _inproj(x_all, mod3, ng0, w_in_bf, n_prompt_tokens, sample_len):
    t = x_all.shape[0]
    n_out = w_in_bf.shape[1]
    tm, tn = 512, 1024
    row = functools.partial(_mod_row, tile_tokens=tm, n_prompt_tokens=n_prompt_tokens,
                            sample_len=sample_len)
    return pl.pallas_call(
        _inproj_kernel,
        grid=(t // tm, n_out // tn),
        in_specs=[pl.BlockSpec((tm, D_MODEL), lambda i, j: (i, 0)),
                  pl.BlockSpec((1, 6, D_MODEL), lambda i, j: (row(i), 0, 0)),
                  pl.BlockSpec((1, D_MODEL), lambda i, j: (0, 0)),
                  pl.BlockSpec((D_MODEL, tn), lambda i, j: (0, j))],
        out_specs=pl.BlockSpec((tm, tn), lambda i, j: (i, j)),
        out_shape=jax.ShapeDtypeStruct((t, n_out), F32),
        scratch_shapes=[pltpu.VMEM((tm, D_MODEL), BF16)],
        compiler_params=pltpu.CompilerParams(dimension_semantics=("parallel", "arbitrary"),
                                             vmem_limit_bytes=VMEM_LIMIT),
        name="inproj",
    )(x_all, mod3, ng0, w_in_bf)


def _hgrn_consts():
    c = CHUNK
    t = np.arange(c)[:, None]
    r = np.arange(c)[None, :]
    a_list, m_list = [], []
    for lvl in range(N_LEVELS):
        b = 1 << lvl
        t_base = (t // (2 * b)) * (2 * b)
        bound = t_base + b - 1
        t_right = (t - t_base) >= b
        a_right = (r > bound) & (r <= t)
        a_left = (r > t) & (r <= bound)
        a_list.append(np.where(t_right, a_right, a_left))
        r_base = (r // (2 * b)) * (2 * b)
        m_list.append((t_base == r_base) & t_right & ((r - r_base) < b))
    a_list.append(r <= t)
    a_list.append(r > t)
    m_list.append(t == r)
    a_f = np.stack(a_list).astype(np.float32)
    m_f = np.stack(m_list).astype(np.float32)
    a_b = a_f[:, ::-1, ::-1]
    m_b = m_f[:, ::-1, ::-1]
    a = np.stack([a_f, a_b]).reshape(2, (N_LEVELS + 2) * c, c)
    m = np.stack([m_f, m_b])
    return a, m


def _hgrn_kernel(*refs, n_chunks, has_s0, emit_state):
    q_ref, ff_ref, fb_ref, iv_ref, og_ref, lbl_ref, gn_ref, a_ref, m_ref = refs[:9]
    pos = 9
    s0_ref = None
    if has_s0:
        s0_ref = refs[pos]
        pos += 1
    o_ref = refs[pos]
    pos += 1
    st_ref = None
    if emit_state:
        st_ref = refs[pos]
        pos += 1
    of_scr, ob_scr, s_scr = refs[pos:pos + 3]
    c = CHUNK

    logits = lbl_ref[...]
    e = jnp.exp(logits - jnp.max(logits, axis=0, keepdims=True))
    lb = e[0] / jnp.sum(e, axis=0)

    for d in range(2):
        if has_s0:
            s_scr[d] = s0_ref[0, 0, d, 0].T
        else:
            s_scr[d] = jnp.zeros((HEAD_DIM, HEAD_DIM), F32)

    def one_direction(d, ci):
        rows = pl.ds(pl.multiple_of(ci * c, c), c)
        qv = q_ref[rows, :]
        qs = qv * _sigmoid(qv)
        z = (ff_ref if d == 0 else fb_ref)[rows, :]
        lbd = lb[d:d + 1, :]
        f = lbd + (1.0 - lbd) * _sigmoid(z)
        g = jnp.log(f)
        k = (1.0 - lbd) * _sigmoid(-z)
        v = iv_ref[rows, :]
        a = a_ref[d]
        g1, g2, g3 = _split3(g)
        x = jnp.exp(_dot(a, g1) + _dot(a, g2) + _dot(a, g3))
        att = m_ref[d, N_LEVELS] * _dot_nt(qs.astype(BF16), k.astype(BF16))
        for lvl in range(N_LEVELS):
            xl = x[lvl * c:(lvl + 1) * c, :]
            att = att + m_ref[d, lvl] * _dot_nt((qs * xl).astype(BF16), (k * xl).astype(BF16))
        x_cum = x[N_LEVELS * c:(N_LEVELS + 1) * c, :]
        x_rev = x[(N_LEVELS + 1) * c:, :]
        st = s_scr[d]
        o = _dot(att.astype(BF16), v.astype(BF16)) + _dot_nt((qs * x_cum).astype(BF16), st.astype(BF16))
        (of_scr if d == 0 else ob_scr)[rows, :] = o
        total = x_cum[c - 1:c, :] if d == 0 else x_cum[0:1, :]
        vh, vl = _split2(v)
        kh, kl = _split2(k * x_rev)
        s_scr[d] = total * st + (_dot_tn(vh, kh) + _dot_tn(vh, kl) + _dot_tn(vl, kh))

    def body(ci, carry):
        one_direction(0, ci)
        one_direction(1, n_chunks - 1 - ci)
        return carry

    lax.fori_loop(0, n_chunks, body, 0)

    o = of_scr[...] + ob_scr[...]
    og = og_ref[...]
    o_ref[...] = _rms_rows(o, gn_ref[...]) * (og * _sigmoid(og))
    if emit_state:
        for d in range(2):
            st_ref[0, 0, d, 0] = s_scr[d].T


def _hgrn(proj, row_block0, n_seq, seq_len, lb_logits, gnorm, a_c, m_c, s0, emit_state, oa_prev):
    t = proj.shape[0]
    nl = lb_logits.shape[0]
    has_s0 = s0 is not None

    def col(cb):
        return pl.BlockSpec((seq_len, HEAD_DIM), lambda b, h: (row_block0 + b, cb * HEADS + h))

    in_specs = [col(0), col(1), col(2), col(3), col(4),
                pl.BlockSpec((nl, 2, HEAD_DIM), lambda b, h: (0, 0, h)),
                pl.BlockSpec((1, HEAD_DIM), lambda b, h: (0, 0)),
                pl.BlockSpec(a_c.shape, lambda b, h: (0, 0, 0)),
                pl.BlockSpec(m_c.shape, lambda b, h: (0, 0, 0, 0))]
    args = [proj, proj, proj, proj, proj, lb_logits, gnorm, a_c, m_c]
    state_block = (1, 1, 2, 1, HEAD_DIM, HEAD_DIM)
    if has_s0:
        in_specs.append(pl.BlockSpec(state_block, lambda b, h: (b, 0, 0, h, 0, 0)))
        args.append(s0)
    in_specs.append(pl.BlockSpec(memory_space=pl.ANY))
    args.append(oa_prev)
    out_specs = [pl.BlockSpec((seq_len, HEAD_DIM), lambda b, h: (row_block0 + b, h))]
    out_shape = [jax.ShapeDtypeStruct((t, D_MODEL), F32)]
    if emit_state:
        out_specs.append(pl.BlockSpec(state_block, lambda b, h: (b, 0, 0, h, 0, 0)))
        out_shape.append(jax.ShapeDtypeStruct((n_seq, 1, 2, HEADS, HEAD_DIM, HEAD_DIM), F32))

    def kern(*refs):
        n_in = len(args)
        refs = refs[:n_in - 1] + refs[n_in:]
        _hgrn_kernel(*refs, n_chunks=seq_len // CHUNK, has_s0=has_s0, emit_state=emit_state)

    outs = pl.pallas_call(
        kern,
        grid=(n_seq, HEADS),
        in_specs=in_specs,
        out_specs=out_specs,
        out_shape=out_shape,
        scratch_shapes=[pltpu.VMEM((seq_len, HEAD_DIM), F32),
                        pltpu.VMEM((seq_len, HEAD_DIM), F32),
                        pltpu.VMEM((2, HEAD_DIM, HEAD_DIM), F32)],
        input_output_aliases={len(args) - 1: 0},
        compiler_params=pltpu.CompilerParams(dimension_semantics=("parallel", "parallel"),
                                             vmem_limit_bytes=VMEM_LIMIT),
        name="hgrn_ctx" if emit_state else "hgrn_lat",
    )(*args)
    return outs


def _mix_kernel(x_ref, oa_ref, ga_ref, gb_ref, gta_ref, gtb_ref, gap_ref, gan_ref, gbp_ref, gbn_ref,
                mod_ref, ng_ref, cw_ref, cb_ref, lng_ref, lnb_ref, wco_ref, who_ref, wmo_ref,
                x1_ref, h2_ref, u_scr, *, n_prompt_tiles, tiles_per_sample):
    tm = MIX_TILE
    hl = CONV_HALO
    i = pl.program_id(0)
    j = lax.rem(jnp.maximum(i - n_prompt_tiles, 0), tiles_per_sample)
    is_prompt = i < n_prompt_tiles
    keep_prev = jnp.where(jnp.logical_or(is_prompt, j == 0), 0.0, 1.0)
    keep_next = jnp.where(jnp.logical_or(is_prompt, j == tiles_per_sample - 1), 0.0, 1.0)

    u_scr[0:hl, :] = gap_ref[...] * _sigmoid(gbp_ref[...]) * keep_prev
    u_scr[hl:hl + tm, :] = ga_ref[...] * _sigmoid(gb_ref[...])
    u_scr[hl + tm:hl + tm + hl, :] = gan_ref[...] * _sigmoid(gbn_ref[...]) * keep_next

    acc = jnp.broadcast_to(cb_ref[...], (tm, D_MODEL))
    off0 = hl - CONV_K // 2
    for k in range(CONV_K):
        acc = acc + cw_ref[k:k + 1, :] * u_scr[off0 + k:off0 + k + tm, :]

    mu = jnp.mean(acc, axis=-1, keepdims=True)
    cen = acc - mu
    var = jnp.mean(cen * cen, axis=-1, keepdims=True)
    y = cen * lax.rsqrt(var + EPS) * lng_ref[...] + lnb_ref[...]
    u2 = y * _sigmoid(y)
    y_b = _dot(u2.astype(BF16), wco_ref[...])
    y_a = _dot(oa_ref[...].astype(BF16), who_ref[...])
    mix = _sigmoid(gta_ref[...]) * y_a + _sigmoid(gtb_ref[...]) * y_b
    x1 = x_ref[...] + mod_ref[0, 2:3, :] * _dot(mix.astype(BF16), wmo_ref[...])
    x1_ref[...] = x1
    h2_ref[...] = _rms_rows(x1, ng_ref[...]) * (1.0 + mod_ref[0, 4:5, :]) + mod_ref[0, 3:4, :]


def _mix(x_all, oa, proj, mod3, ng1, conv_w, conv_b, ln_g, ln_b, wco, who, wmo,
         n_prompt_tokens, sample_len):
    t = x_all.shape[0]
    tm = MIX_TILE
    hl = CONV_HALO
    r = tm // hl
    n_halo_blocks = t // hl
    n_col = D_MODEL // D_MODEL
    col0 = 5 * HEADS * HEAD_DIM // D_MODEL
    row = functools.partial(_mod_row, tile_tokens=tm, n_prompt_tokens=n_prompt_tokens,
                            sample_len=sample_len)

    def tile(cb):
        return pl.BlockSpec((tm, D_MODEL), lambda i: (i, cb))

    def prev(cb):
        return pl.BlockSpec((hl, D_MODEL), lambda i: (jnp.maximum(i * r - 1, 0), cb))

    def nxt(cb):
        return pl.BlockSpec((hl, D_MODEL), lambda i: (jnp.minimum((i + 1) * r, n_halo_blocks - 1), cb))

    def full(shape):
        return pl.BlockSpec(shape, lambda i: (0,) * len(shape))

    kern = functools.partial(_mix_kernel, n_prompt_tiles=n_prompt_tokens // tm,
                             tiles_per_sample=sample_len // tm)
    del n_col
    return pl.pallas_call(
        kern,
        grid=(t // tm,),
        in_specs=[tile(0), tile(0), tile(col0), tile(col0 + 1), tile(col0 + 2), tile(col0 + 3),
                  prev(col0), nxt(col0), prev(col0 + 1), nxt(col0 + 1),
                  pl.BlockSpec((1, 6, D_MODEL), lambda i: (row(i), 0, 0)),
                  full((1, D_MODEL)), full((CONV_K, D_MODEL)), full((1, D_MODEL)),
                  full((1, D_MODEL)), full((1, D_MODEL)),
                  full((D_MODEL, D_MODEL)), full((D_MODEL, D_MODEL)), full((D_MODEL, D_MODEL))],
        out_specs=[tile(0), tile(0)],
        out_shape=[jax.ShapeDtypeStruct((t, D_MODEL), F32), jax.ShapeDtypeStruct((t, D_MODEL), F32)],
        scratch_shapes=[pltpu.VMEM((tm + 2 * hl, D_MODEL), F32)],
        compiler_params=pltpu.CompilerParams(dimension_semantics=("parallel",),
                                             vmem_limit_bytes=VMEM_LIMIT),
        name="mix",
    )(x_all, oa, proj, proj, proj, proj, proj, proj, proj, proj, mod3, ng1, conv_w, conv_b, ln_g, ln_b,
      wco, who, wmo)


def _cand_rows():
    return [(a, PEER_TOPK // (a + 1)) for a in range(PEER_TOPK)]


N_CAND = sum(nb for _, nb in _cand_rows())
N_CAND_PAD = -(-N_CAND // 8) * 8


def _topk_kernel(h2_ref, wqh_ref, wql_ref, keys_ref, rank_ref, e2_ref, n_ref, b_ref,
                 q_scr, v_scr, cand_scr):
    nk = PEER_NKEYS
    hh, hl = _split2(h2_ref[...])
    wqh = wqh_ref[...]
    q_scr[...] = _dot_nt(wqh, hh) + _dot_nt(wqh, hl) + _dot_nt(wql_ref[...], hh)
    neg = -jnp.inf

    def head(h, carry):
        s = []
        for p in range(2):
            kh, kl = _split2(keys_ref[h, p])
            qh, ql = _split2(q_scr[pl.ds(pl.multiple_of(h * 2 * nk + p * nk, nk), nk), :])
            sp = _dot(kh, qh) + _dot(kh, ql) + _dot(kl, qh)
            s.append(sp)
            cur = sp
            rank = jnp.full(sp.shape, float(PEER_TOPK), F32)
            for a in range(PEER_TOPK):
                m = jnp.max(cur, axis=0, keepdims=True)
                hit = cur == m
                rank = jnp.where(hit, float(a), rank)
                cur = jnp.where(hit, neg, cur)
                v_scr[p, a:a + 1, :] = m
            if p == 1:
                rank_ref[h] = rank
        s1, s2 = s
        v1 = v_scr[0]
        v2 = v_scr[1]
        off = 0
        for a, nb in _cand_rows():
            cand_scr[off:off + nb, :] = v1[a:a + 1, :] + v2[0:nb, :]
            off += nb
        if N_CAND_PAD > N_CAND:
            cand_scr[N_CAND:N_CAND_PAD, :] = jnp.full((N_CAND_PAD - N_CAND, s1.shape[1]), neg, F32)
        cand = cand_scr[...]
        cur = cand
        tau = None
        for a in range(PEER_TOPK):
            tau = jnp.max(cur, axis=0, keepdims=True)
            if a < PEER_TOPK - 1:
                cur = jnp.where(cur == tau, neg, cur)
        top = v1[0:1, :] + v2[0:1, :]
        zsum = jnp.sum(jnp.where(cand >= tau, jnp.exp(cand - top), 0.0), axis=0, keepdims=True)
        cnt = jnp.zeros(s1.shape, F32)
        for b in range(PEER_TOPK):
            cnt = cnt + jnp.where(s1 + v2[b:b + 1, :] >= tau, 1.0, 0.0)
        n_ref[h] = cnt
        b_ref[h] = jnp.exp(s1 - v1[0:1, :]) / zsum
        e2_ref[h] = jnp.exp(s2 - v2[0:1, :])
        return carry

    lax.fori_loop(0, PEER_HEADS, head, 0)


def _topk(h2, wq_t_hi, wq_t_lo, keys):
    t = h2.shape[0]
    tm = 256
    nq = wq_t_hi.shape[0]
    route = pl.BlockSpec((PEER_HEADS, PEER_NKEYS, tm), lambda i: (0, 0, i))
    route_shape = jax.ShapeDtypeStruct((PEER_HEADS, PEER_NKEYS, t), F32)
    return pl.pallas_call(
        _topk_kernel,
        grid=(t // tm,),
        in_specs=[pl.BlockSpec((tm, D_MODEL), lambda i: (i, 0)),
                  pl.BlockSpec((nq, D_MODEL), lambda i: (0, 0)),
                  pl.BlockSpec((nq, D_MODEL), lambda i: (0, 0)),
                  pl.BlockSpec(keys.shape, lambda i: (0, 0, 0, 0))],
        out_specs=[route, route, route, route],
        out_shape=[route_shape] * 4,
        scratch_shapes=[pltpu.VMEM((nq, tm), F32),
                        pltpu.VMEM((2, PEER_TOPK, tm), F32),
                        pltpu.VMEM((N_CAND_PAD, tm), F32)],
        compiler_params=pltpu.CompilerParams(dimension_semantics=("parallel",),
                                             vmem_limit_bytes=VMEM_LIMIT),
        name="peer_topk",
    )(h2, wq_t_hi, wq_t_lo, keys)


def _peer_kernel(h2_ref, x1_ref, mod_ref, fg_ref, rank_ref, e2_ref, n_ref, b_ref, u_ref, vt_ref,
                 y_ref, acc_scr, xb_scr, g_scr, *, rows_per_block):
    nk = PEER_NKEYS
    e = pl.program_id(1)

    @pl.when(e == 0)
    def _():
        acc_scr[...] = jnp.zeros(acc_scr.shape, F32)
        xb_scr[...] = h2_ref[...].astype(BF16)

    xb = xb_scr[...]
    for ii in range(rows_per_block):
        irow = e * rows_per_block + ii
        a = _dot_nt(u_ref[ii * nk:(ii + 1) * nk, :], xb)
        w = jnp.zeros(a.shape, F32)
        for h in range(PEER_HEADS):
            cnt = n_ref[h, pl.ds(irow, 1), :]
            bb = b_ref[h, pl.ds(irow, 1), :]
            w = w + jnp.where(rank_ref[h] < cnt, e2_ref[h] * bb, 0.0)
        gel = 0.5 * a * (1.0 + lax.erf(a * 0.7071067811865476))
        g_scr[ii * nk:(ii + 1) * nk, :] = (w * gel).astype(BF16)
    acc_scr[...] += _dot(vt_ref[...], g_scr[...])

    @pl.when(e == pl.num_programs(1) - 1)
    def _():
        x2 = x1_ref[...] + mod_ref[0, 5:6, :] * acc_scr[...].T
        y_ref[...] = _rms_rows(x2, fg_ref[...])


def _peer(h2, x1, mod3, final_g, rank2, e2, cnt, bcoef, u_bf, vt_bf, n_prompt_tokens, sample_len):
    t = h2.shape[0]
    tm = 512
    rows_per_block = 4
    nb = rows_per_block * PEER_NKEYS
    n_exp = u_bf.shape[0]
    row = functools.partial(_mod_row, tile_tokens=tm, n_prompt_tokens=n_prompt_tokens,
                            sample_len=sample_len)
    route = pl.BlockSpec((PEER_HEADS, PEER_NKEYS, tm), lambda i, e: (0, 0, i))
    tok = pl.BlockSpec((tm, D_MODEL), lambda i, e: (i, 0))
    kern = functools.partial(_peer_kernel, rows_per_block=rows_per_block)
    return pl.pallas_call(
        kern,
        grid=(t // tm, n_exp // nb),
        in_specs=[tok, tok,
                  pl.BlockSpec((1, 6, D_MODEL), lambda i, e: (row(i), 0, 0)),
                  pl.BlockSpec((1, D_MODEL), lambda i, e: (0, 0)),
                  route, route, route, route,
                  pl.BlockSpec((nb, D_MODEL), lambda i, e: (e, 0)),
                  pl.BlockSpec((D_MODEL, nb), lambda i, e: (0, e))],
        out_specs=tok,
        out_shape=jax.ShapeDtypeStruct((t, D_MODEL), F32),
        scratch_shapes=[pltpu.VMEM((D_MODEL, tm), F32),
                        pltpu.VMEM((tm, D_MODEL), BF16),
                        pltpu.VMEM((nb, tm), BF16)],
        compiler_params=pltpu.CompilerParams(dimension_semantics=("parallel", "arbitrary"),
                                             vmem_limit_bytes=VMEM_LIMIT),
        name="peer_experts",
    )(h2, x1, mod3, final_g, rank2, e2, cnt, bcoef, u_bf, vt_bf)


def kernel(x_prompt, x_sample, state_hgrn, c, c_ctx, norm_g, w_ada, b_ada, w_in, hgrn_lb_logits, hgrn_gnorm_g, w_hgrn_out, conv_w, conv_b, conv_ln_g, conv_ln_b, w_conv_out, w_mix_out, peer_wq, peer_keys, peer_u, peer_v, final_g):
    n_prompt, prompt_len, d = x_prompt.shape
    n_sample, sample_len, _ = x_sample.shape
    depth = w_in.shape[0]
    assert d == D_MODEL and depth == 1 and prompt_len == MIX_TILE
    assert n_sample + 1 <= 8 and sample_len % MIX_TILE == 0
    n_prompt_tokens = n_prompt * prompt_len
    assert n_prompt_tokens % sample_len == 0
    layer = 0

    x_all = jnp.concatenate([x_prompt.reshape(n_prompt_tokens, d),
                             x_sample.reshape(n_sample * sample_len, d)], axis=0)

    cond8 = jnp.zeros((8, d), F32).at[0].set(c_ctx).at[1:1 + n_sample].set(c)
    mod3 = _ada(cond8, w_ada[layer], b_ada[layer]).reshape(8, 6, d)

    proj = _inproj(x_all, mod3, norm_g[layer, 0:1], w_in[layer].astype(BF16),
                   n_prompt_tokens, sample_len)

    a_np, m_np = _hgrn_consts()
    a_c = jnp.asarray(a_np, BF16)
    m_c = jnp.asarray(m_np, F32)
    gnorm = hgrn_gnorm_g[layer].reshape(1, HEAD_DIM)
    oa0 = jnp.zeros((x_all.shape[0], d), F32)
    oa1, new_state = _hgrn(proj, 0, n_prompt, prompt_len, hgrn_lb_logits, gnorm, a_c, m_c,
                           None, True, oa0)
    (oa,) = _hgrn(proj, n_prompt_tokens // sample_len, n_sample, sample_len, hgrn_lb_logits, gnorm,
                  a_c, m_c, state_hgrn, False, oa1)

    row = lambda v: v.reshape(1, d)
    x1, h2 = _mix(x_all, oa, proj, mod3, norm_g[layer, 1:2], conv_w[layer], row(conv_b[layer]),
                  row(conv_ln_g[layer]), row(conv_ln_b[layer]), w_conv_out[layer].astype(BF16),
                  w_hgrn_out[layer].astype(BF16), w_mix_out[layer].astype(BF16),
                  n_prompt_tokens, sample_len)

    wq_t = peer_wq[layer].T
    wq_hi = wq_t.astype(BF16)
    wq_lo = (wq_t - wq_hi.astype(F32)).astype(BF16)
    rank2, e2, cnt, bcoef = _topk(h2, wq_hi, wq_lo, peer_keys[layer])

    y = _peer(h2, x1, mod3, row(final_g), rank2, e2, cnt, bcoef,
              peer_u[layer].astype(BF16), peer_v[layer].T.astype(BF16), n_prompt_tokens, sample_len)

    y_prompt = y[:n_prompt_tokens].reshape(n_prompt, prompt_len, d)
    y_sample = y[n_prompt_tokens:].reshape(n_sample, sample_len, d)
    return y_prompt, y_sample, new_state
```

```python
import functools

import numpy as np
import jax
import jax.numpy as jnp
from jax import lax
from jax.experimental import pallas as pl
from jax.experimental.pallas import tpu as pltpu

D_MODEL = 1024
LANES = 128
HEADS = 8
HEAD_DIM = 128
HEADS_PER_STEP = 2
CONV_K = 31
CONV_HALO = 16
CONV_ROWS = 64
PEER_HEADS = 8
PEER_NKEYS = 128
PEER_TOPK = 16
EPS = 1e-6
CHUNK = 128
N_LEVELS = 7
MIX_TILE = 256
N_GATE_BLOCKS = 2
VMEM_LIMIT = 56 * 1024 * 1024

F32 = jnp.float32
BF16 = jnp.bfloat16


def _dot(a, b):
    return jnp.dot(a, b, preferred_element_type=F32)


def _dot_nt(a, b):
    return lax.dot_general(a, b, (((1,), (1,)), ((), ())), preferred_element_type=F32)


def _dot_tn(a, b):
    return lax.dot_general(a, b, (((0,), (0,)), ((), ())), preferred_element_type=F32)


def _split2(x):
    hi = x.astype(BF16)
    lo = (x - hi.astype(F32)).astype(BF16)
    return hi, lo


def _pack_pairs(x):
    return pltpu.bitcast(x.astype(BF16), jnp.uint32)


def _unpack_pairs(x):
    return pltpu.bitcast(x, BF16)


def _sigmoid(x):
    return 1.0 / (1.0 + jnp.exp(-x))


def _rms_rows(x, g):
    ms = jnp.mean(x * x, axis=-1, keepdims=True)
    return x * lax.rsqrt(ms + EPS) * g


def _mod_row(tile, tile_tokens, n_prompt_tokens, sample_len):
    npt = n_prompt_tokens // tile_tokens
    tps = sample_len // tile_tokens
    return jnp.where(tile < npt, 0, 1 + (tile - npt) // tps)


def _ada_kernel(cond_ref, w_ref, b_ref, o_ref):
    c = cond_ref[...]
    s = c * _sigmoid(c)
    o_ref[...] = jnp.dot(s, w_ref[...], preferred_element_type=F32,
                         precision=lax.Precision.HIGHEST) + b_ref[...]


def _ada(cond8, w_ada, b_ada):
    n_out = w_ada.shape[1]
    bn = D_MODEL
    return pl.pallas_call(
        _ada_kernel,
        grid=(n_out // bn,),
        in_specs=[pl.BlockSpec((8, D_MODEL), lambda j: (0, 0)),
                  pl.BlockSpec((D_MODEL, bn), lambda j: (0, j)),
                  pl.BlockSpec((1, bn), lambda j: (0, j))],
        out_specs=pl.BlockSpec((8, bn), lambda j: (0, j)),
        out_shape=jax.ShapeDtypeStruct((8, n_out), F32),
        compiler_params=pltpu.CompilerParams(dimension_semantics=("parallel",),
                                             vmem_limit_bytes=VMEM_LIMIT),
        name="ada",
    )(cond8, w_ada, b_ada.reshape(1, n_out))


def _inproj_kernel(x_ref, mod_ref, ng_ref, w_ref, gate_ref, rest_ref, h_scr):
    j = pl.program_id(1)

    @pl.when(j == 0)
    def _():
        y = _rms_rows(x_ref[...], ng_ref[...])
        h = y * (1.0 + mod_ref[0, 1:2, :]) + mod_ref[0, 0:1, :]
        h_scr[...] = h.astype(BF16)

    y = _dot(h_scr[...], w_ref[...])

    @pl.when(j < N_GATE_BLOCKS)
    def _():
        gate_ref[...] = y

    @pl.when(j >= N_GATE_BLOCKS)
    def _():
        rest_ref[...] = y.astype(BF16)


def _inproj(x_all, mod3, ng0, w_perm_bf, n_prompt_tokens, sample_len):
    t = x_all.shape[0]
    n_out = w_perm_bf.shape[1]
    tm, tn = 1024, D_MODEL
    nj = n_out // tn
    row = functools.partial(_mod_row, tile_tokens=tm, n_prompt_tokens=n_prompt_tokens,
                            sample_len=sample_len)
    return pl.pallas_call(
        _inproj_kernel,
        grid=(t // tm, nj),
        in_specs=[pl.BlockSpec((tm, D_MODEL), lambda i, j: (i, 0)),
                  pl.BlockSpec((1, 6, D_MODEL), lambda i, j: (row(i), 0, 0)),
                  pl.BlockSpec((1, D_MODEL), lambda i, j: (0, 0)),
                  pl.BlockSpec((D_MODEL, tn), lambda i, j: (0, j))],
        out_specs=[pl.BlockSpec((tm, tn), lambda i, j: (i, jnp.minimum(j, N_GATE_BLOCKS - 1))),
                   pl.BlockSpec((tm, tn), lambda i, j: (i, jnp.maximum(j - N_GATE_BLOCKS, 0)))],
        out_shape=[jax.ShapeDtypeStruct((t, N_GATE_BLOCKS * tn), F32),
                   jax.ShapeDtypeStruct((t, n_out - N_GATE_BLOCKS * tn), BF16)],
        scratch_shapes=[pltpu.VMEM((tm, D_MODEL), BF16)],
        compiler_params=pltpu.CompilerParams(dimension_semantics=("parallel", "arbitrary"),
                                             vmem_limit_bytes=VMEM_LIMIT),
        name="inproj",
    )(x_all, mod3, ng0, w_perm_bf)


def _hgrn_consts():
    c = CHUNK
    t = np.arange(c)[:, None]
    r = np.arange(c)[None, :]
    a_list, m_list = [], []
    for lvl in range(N_LEVELS):
        b = 1 << lvl
        t_base = (t // (2 * b)) * (2 * b)
        bound = t_base + b - 1
        t_right = (t - t_base) >= b
        a_right = (r > bound) & (r <= t)
        a_left = (r > t) & (r <= bound)
        a_list.append(np.where(t_right, a_right, a_left))
        r_base = (r // (2 * b)) * (2 * b)
        m_list.append((t_base == r_base) & t_right & ((r - r_base) < b))
    a_list.append(r <= t)
    a_list.append(r > t)
    m_list.append(t == r)
    a_f = np.stack(a_list).astype(np.float32)
    m_f = np.stack(m_list).astype(np.float32)
    a_b = a_f[:, ::-1, ::-1]
    m_b = m_f[:, ::-1, ::-1]
    a = np.stack([a_f, a_b]).reshape(2, (N_LEVELS + 2) * c, c)
    m = np.stack([m_f, m_b])
    return a, m


def _hgrn_kernel(*refs, n_chunks, has_s0, emit_state):
    q_ref, ff_ref, fb_ref, iv_ref, og_ref, lbl_ref, gn_ref, a_ref, m_ref = refs[:9]
    pos = 9
    s0_ref = None
    if has_s0:
        s0_ref = refs[pos]
        pos += 1
    o_ref = refs[pos]
    pos += 1
    st_ref = None
    if emit_state:
        st_ref = refs[pos]
        pos += 1
    of_scr, ob_scr, s_scr = refs[pos:pos + 3]
    c = CHUNK
    hd = HEAD_DIM
    nh = HEADS_PER_STEP

    logits = lbl_ref[...]
    e = jnp.exp(logits - jnp.max(logits, axis=0, keepdims=True))
    lb = e[0] / jnp.sum(e, axis=0)

    for d in range(2):
        for hh in range(nh):
            if has_s0:
                s_scr[d, hh] = s0_ref[0, 0, d, hh].T
            else:
                s_scr[d, hh] = jnp.zeros((hd, hd), F32)

    def one_direction(d, ci):
        rows = pl.ds(pl.multiple_of(ci * c, c), c)
        qv = q_ref[rows, :].astype(F32)
        qs = qv * _sigmoid(qv)
        z = (ff_ref if d == 0 else fb_ref)[rows, :]
        lbd = lb[d:d + 1, :]
        f = lbd + (1.0 - lbd) * _sigmoid(z)
        g = jnp.log(f)
        k = (1.0 - lbd) * _sigmoid(-z)
        v = iv_ref[rows, :]
        a = a_ref[d]
        g1, g2 = _split2(g)
        x = jnp.exp(_dot(a, g1) + _dot(a, g2))
        x_cum = x[N_LEVELS * c:(N_LEVELS + 1) * c, :]
        x_rev = x[(N_LEVELS + 1) * c:, :]
        qb = qs.astype(BF16)
        kb = k.astype(BF16)
        q_cum = (qs * x_cum).astype(BF16)
        kh, kl = _split2(k * x_rev)
        ql = [(qs * x[lvl * c:(lvl + 1) * c, :]).astype(BF16) for lvl in range(N_LEVELS)]
        kl_ = [(k * x[lvl * c:(lvl + 1) * c, :]).astype(BF16) for lvl in range(N_LEVELS)]
        for hh in range(nh):
            cols = slice(hh * hd, (hh + 1) * hd)
            att = m_ref[d, N_LEVELS] * _dot_nt(qb[:, cols], kb[:, cols])
            for lvl in range(N_LEVELS):
                att = att + m_ref[d, lvl] * _dot_nt(ql[lvl][:, cols], kl_[lvl][:, cols])
            st = s_scr[d, hh]
            vh = v[:, cols]
            o = _dot(att.astype(BF16), vh) + _dot_nt(q_cum[:, cols], st.astype(BF16))
            (of_scr if d == 0 else ob_scr)[rows, cols] = o
            total = x_cum[c - 1:c, cols] if d == 0 else x_cum[0:1, cols]
            s_scr[d, hh] = total * st + (_dot_tn(vh, kh[:, cols]) + _dot_tn(vh, kl[:, cols]))

    def body(ci, carry):
        one_direction(0, ci)
        one_direction(1, n_chunks - 1 - ci)
        return carry

    lax.fori_loop(0, n_chunks, body, 0)

    og = og_ref[...].astype(F32)
    gate = og * _sigmoid(og)
    for hh in range(nh):
        cols = slice(hh * hd, (hh + 1) * hd)
        o = of_scr[:, cols] + ob_scr[:, cols]
        o_ref[:, cols] = (_rms_rows(o, gn_ref[...]) * gate[:, cols]).astype(BF16)
    if emit_state:
        for d in range(2):
            for hh in range(nh):
                st_ref[0, 0, d, hh] = s_scr[d, hh].T


def _hgrn(gates, rest, row_block0, n_seq, seq_len, lb_logits, gnorm, a_c, m_c, s0, emit_state, oa_prev):
    t = gates.shape[0]
    nl = lb_logits.shape[0]
    has_s0 = s0 is not None
    nh = HEADS_PER_STEP
    w = nh * HEAD_DIM
    per = D_MODEL // w

    def col(cb):
        return pl.BlockSpec((seq_len, w), lambda b, h: (row_block0 + b, cb * per + h))

    in_specs = [col(0), col(0), col(1), col(1), col(2),
                pl.BlockSpec((nl, 2, w), lambda b, h: (0, 0, h)),
                pl.BlockSpec((1, HEAD_DIM), lambda b, h: (0, 0)),
                pl.BlockSpec(a_c.shape, lambda b, h: (0, 0, 0)),
                pl.BlockSpec(m_c.shape, lambda b, h: (0, 0, 0, 0))]
    args = [rest, gates, gates, rest, rest, lb_logits, gnorm, a_c, m_c]
    state_block = (1, 1, 2, nh, HEAD_DIM, HEAD_DIM)
    if has_s0:
        in_specs.append(pl.BlockSpec(state_block, lambda b, h: (b, 0, 0, h, 0, 0)))
        args.append(s0)
    in_specs.append(pl.BlockSpec(memory_space=pl.ANY))
    args.append(oa_prev)
    out_specs = [pl.BlockSpec((seq_len, w), lambda b, h: (row_block0 + b, h))]
    out_shape = [jax.ShapeDtypeStruct((t, D_MODEL), BF16)]
    if emit_state:
        out_specs.append(pl.BlockSpec(state_block, lambda b, h: (b, 0, 0, h, 0, 0)))
        out_shape.append(jax.ShapeDtypeStruct((n_seq, 1, 2, HEADS, HEAD_DIM, HEAD_DIM), F32))

    def kern(*refs):
        n_in = len(args)
        refs = refs[:n_in - 1] + refs[n_in:]
        _hgrn_kernel(*refs, n_chunks=seq_len // CHUNK, has_s0=has_s0, emit_state=emit_state)

    outs = pl.pallas_call(
        kern,
        grid=(n_seq, HEADS // nh),
        in_specs=in_specs,
        out_specs=out_specs,
        out_shape=out_shape,
        scratch_shapes=[pltpu.VMEM((seq_len, w), F32),
                        pltpu.VMEM((seq_len, w), F32),
                        pltpu.VMEM((2, nh, HEAD_DIM, HEAD_DIM), F32)],
        input_output_aliases={len(args) - 1: 0},
        compiler_params=pltpu.CompilerParams(dimension_semantics=("parallel", "parallel"),
                                             vmem_limit_bytes=VMEM_LIMIT),
        name="hgrn_ctx" if emit_state else "hgrn_lat",
    )(*args)
    return outs


def _mix_kernel(x_ref, oa_ref, ga_ref, gb_ref, gta_ref, gtb_ref, gap_ref, gan_ref, gbp_ref, gbn_ref,
                mod_ref, ng_ref, cw_ref, cb_ref, lng_ref, lnb_ref, wco_ref, who_ref, wmo_ref,
                x1_ref, h2_ref, u_scr, conv_scr, *, n_prompt_tiles, tiles_per_sample):
    tm = MIX_TILE
    hl = CONV_HALO
    i = pl.program_id(0)
    j = lax.rem(jnp.maximum(i - n_prompt_tiles, 0), tiles_per_sample)
    is_prompt = i < n_prompt_tiles
    keep_prev = jnp.where(jnp.logical_or(is_prompt, j == 0), 0.0, 1.0)
    keep_next = jnp.where(jnp.logical_or(is_prompt, j == tiles_per_sample - 1), 0.0, 1.0)

    def glu(a_ref, b_ref):
        return a_ref[...].astype(F32) * _sigmoid(b_ref[...].astype(F32))

    u_scr[0:hl, :] = glu(gap_ref, gbp_ref) * keep_prev
    u_scr[hl:hl + tm, :] = glu(ga_ref, gb_ref)
    u_scr[hl + tm:hl + tm + hl, :] = glu(gan_ref, gbn_ref) * keep_next

    off0 = hl - CONV_K // 2
    rb = CONV_ROWS
    span = rb + 8 * (-(-(CONV_K + off0) // 8))
    for cblk in range(D_MODEL // LANES):
        cols = slice(cblk * LANES, (cblk + 1) * LANES)
        wv = cw_ref[:, cols]
        taps = [jnp.broadcast_to(wv[k:k + 1, :], (rb, LANES)) for k in range(CONV_K)]
        bias = jnp.broadcast_to(cb_ref[:, cols], (rb, LANES))
        for r0 in range(0, tm, rb):
            ext = u_scr[r0:r0 + span, cols]
            acc = bias
            for s in range(8):
                us = ext if s == 0 else pltpu.roll(ext, span - s, axis=0)
                for q in range(span // 8):
                    k = 8 * q + s - off0
                    if 0 <= k < CONV_K:
                        acc = acc + taps[k] * us[8 * q:8 * q + rb, :]
            conv_scr[r0:r0 + rb, cols] = acc

    acc = conv_scr[...]
    mu = jnp.mean(acc, axis=-1, keepdims=True)
    cen = acc - mu
    var = jnp.mean(cen * cen, axis=-1, keepdims=True)
    y = cen * lax.rsqrt(var + EPS) * lng_ref[...] + lnb_ref[...]
    u2 = y * _sigmoid(y)
    y_b = _dot(u2.astype(BF16), wco_ref[...])
    y_a = _dot(oa_ref[...], who_ref[...])
    mix = _sigmoid(gta_ref[...].astype(F32)) * y_a + _sigmoid(gtb_ref[...].astype(F32)) * y_b
    x1 = x_ref[...] + mod_ref[0, 2:3, :] * _dot(mix.astype(BF16), wmo_ref[...])
    x1_ref[...] = x1
    h2_ref[...] = _rms_rows(x1, ng_ref[...]) * (1.0 + mod_ref[0, 4:5, :]) + mod_ref[0, 3:4, :]


def _mix(x_all, oa, rest, mod3, ng1, conv_w, conv_b, ln_g, ln_b, wco, who, wmo,
         n_prompt_tokens, sample_len):
    t = x_all.shape[0]
    tm = MIX_TILE
    hl = CONV_HALO
    r = tm // hl
    n_halo_blocks = t // hl
    col0 = 3
    row = functools.partial(_mod_row, tile_tokens=tm, n_prompt_tokens=n_prompt_tokens,
                            sample_len=sample_len)

    def tile(cb):
        return pl.BlockSpec((tm, D_MODEL), lambda i: (i, cb))

    def prev(cb):
        return pl.BlockSpec((hl, D_MODEL), lambda i: (jnp.maximum(i * r - 1, 0), cb))

    def nxt(cb):
        return pl.BlockSpec((hl, D_MODEL), lambda i: (jnp.minimum((i + 1) * r, n_halo_blocks - 1), cb))

    def full(shape):
        return pl.BlockSpec(shape, lambda i: (0,) * len(shape))

    kern = functools.partial(_mix_kernel, n_prompt_tiles=n_prompt_tokens // tm,
                             tiles_per_sample=sample_len // tm)
    return pl.pallas_call(
        kern,
        grid=(t // tm,),
        in_specs=[tile(0), tile(0), tile(col0), tile(col0 + 1), tile(col0 + 2), tile(col0 + 3),
                  prev(col0), nxt(col0), prev(col0 + 1), nxt(col0 + 1),
                  pl.BlockSpec((1, 6, D_MODEL), lambda i: (row(i), 0, 0)),
                  full((1, D_MODEL)), full((CONV_K, D_MODEL)), full((1, D_MODEL)),
                  full((1, D_MODEL)), full((1, D_MODEL)),
                  full((D_MODEL, D_MODEL)), full((D_MODEL, D_MODEL)), full((D_MODEL, D_MODEL))],
        out_specs=[tile(0), tile(0)],
        out_shape=[jax.ShapeDtypeStruct((t, D_MODEL), F32), jax.ShapeDtypeStruct((t, D_MODEL), F32)],
        scratch_shapes=[pltpu.VMEM((tm + 2 * hl, D_MODEL), F32),
                        pltpu.VMEM((tm, D_MODEL), F32)],
        compiler_params=pltpu.CompilerParams(dimension_semantics=("parallel",),
                                             vmem_limit_bytes=VMEM_LIMIT),
        name="mix",
    )(x_all, oa, rest, rest, rest, rest, rest, rest, rest, rest, mod3, ng1, conv_w, conv_b, ln_g, ln_b,
      wco, who, wmo)


def _cand_rows():
    return [(a, PEER_TOPK // (a + 1)) for a in range(PEER_TOPK)]


N_CAND = sum(nb for _, nb in _cand_rows())
N_CAND_PAD = -(-N_CAND // 8) * 8


def _topk_kernel(h2_ref, wqh_ref, wql_ref, keys_ref, rank_ref, e2_ref, n_ref, b_ref,
                 q_scr, v_scr, cand_scr):
    nk = PEER_NKEYS
    tm = h2_ref.shape[0]
    hh, hl = _split2(h2_ref[...])
    wqh = wqh_ref[...]
    q_scr[...] = _dot_nt(wqh, hh) + _dot_nt(wqh, hl) + _dot_nt(wql_ref[...], hh)
    neg = -jnp.inf

    def head(h, carry):
        kparts = [_split2(keys_ref[h, p]) for p in range(2)]
        for lt in range(tm // LANES):
            lanes = slice(lt * LANES, (lt + 1) * LANES)
            s = []
            for p in range(2):
                kh, kl = kparts[p]
                qh, ql = _split2(q_scr[pl.ds(pl.multiple_of(h * 2 * nk + p * nk, nk), nk), lanes])
                sp = _dot(kh, qh) + _dot(kh, ql) + _dot(kl, qh)
                s.append(sp)
                cur = sp
                rank = jnp.full(sp.shape, float(PEER_TOPK), F32)
                for a in range(PEER_TOPK):
                    m = jnp.max(cur, axis=0, keepdims=True)
                    hit = cur == m
                    if p == 1:
                        rank = jnp.where(hit, float(a), rank)
                    cur = jnp.where(hit, neg, cur)
                    v_scr[p, a:a + 1, :] = m
                if p == 1:
                    rank_ref[h, :, lanes] = _pack_pairs(rank)
            s1, s2 = s
            v1 = v_scr[0]
            v2 = v_scr[1]
            off = 0
            for a, nb in _cand_rows():
                cand_scr[off:off + nb, :] = v1[a:a + 1, :] + v2[0:nb, :]
                off += nb
            if N_CAND_PAD > N_CAND:
                cand_scr[N_CAND:N_CAND_PAD, :] = jnp.full((N_CAND_PAD - N_CAND, LANES), neg, F32)
            cand = cand_scr[...]
            cur = cand
            tau = None
            for a in range(PEER_TOPK):
                tau = jnp.max(cur, axis=0, keepdims=True)
                if a < PEER_TOPK - 1:
                    cur = jnp.where(cur == tau, neg, cur)
            top = v1[0:1, :] + v2[0:1, :]
            zsum = jnp.sum(jnp.where(cand >= tau, jnp.exp(cand - top), 0.0), axis=0, keepdims=True)
            cnt = jnp.zeros(s1.shape, F32)
            for b in range(PEER_TOPK):
                cnt = cnt + jnp.where(s1 + v2[b:b + 1, :] >= tau, 1.0, 0.0)
            n_ref[h, :, lanes] = cnt
            b_ref[h, :, lanes] = jnp.exp(s1 - v1[0:1, :]) / zsum
            e2_ref[h, :, lanes] = _pack_pairs(jnp.exp(s2 - v2[0:1, :]))
        return carry

    lax.fori_loop(0, PEER_HEADS, head, 0)


def _topk(h2, wq_t_hi, wq_t_lo, keys):
    t = h2.shape[0]
    tm = 256
    nq = wq_t_hi.shape[0]
    route = pl.BlockSpec((PEER_HEADS, PEER_NKEYS, tm), lambda i: (0, 0, i))
    packed = pl.BlockSpec((PEER_HEADS, PEER_NKEYS // 2, tm), lambda i: (0, 0, i))

    def route_shape(dt):
        return jax.ShapeDtypeStruct((PEER_HEADS, PEER_NKEYS, t), dt)

    packed_shape = jax.ShapeDtypeStruct((PEER_HEADS, PEER_NKEYS // 2, t), jnp.uint32)

    return pl.pallas_call(
        _topk_kernel,
        grid=(t // tm,),
        in_specs=[pl.BlockSpec((tm, D_MODEL), lambda i: (i, 0)),
                  pl.BlockSpec((nq, D_MODEL), lambda i: (0, 0)),
                  pl.BlockSpec((nq, D_MODEL), lambda i: (0, 0)),
                  pl.BlockSpec(keys.shape, lambda i: (0, 0, 0, 0))],
        out_specs=[packed, packed, route, route],
        out_shape=[packed_shape, packed_shape, route_shape(F32), route_shape(F32)],
        scratch_shapes=[pltpu.VMEM((nq, tm), F32),
                        pltpu.VMEM((2, PEER_TOPK, LANES), F32),
                        pltpu.VMEM((N_CAND_PAD, LANES), F32)],
        compiler_params=pltpu.CompilerParams(dimension_semantics=("parallel",),
                                             vmem_limit_bytes=VMEM_LIMIT),
        name="peer_topk",
    )(h2, wq_t_hi, wq_t_lo, keys)


PEER_ROWS_PER_HALF = 4


def _peer_kernel(h2_ref, x1_ref, mod_ref, fg_ref, rank_ref, e2_ref, n_ref, b_ref, u_ref, vta_ref, vtb_ref,
                 y_ref, acc_scr, xb_scr, a0_scr, a1_scr, g0_scr, g1_scr):
    nk = PEER_NKEYS
    rph = PEER_ROWS_PER_HALF
    nbh = rph * nk
    tm = h2_ref.shape[0]
    s = pl.program_id(1)

    @pl.when(s == 0)
    def _():
        acc_scr[...] = jnp.zeros(acc_scr.shape, F32)
        xb_scr[...] = h2_ref[...].astype(BF16)
        g1_scr[...] = jnp.zeros(g1_scr.shape, BF16)

    zero = jnp.zeros((nk, LANES), BF16)

    def packed_row(ref, h, ii, lanes):
        return jnp.broadcast_to(ref[h, ii:ii + 1, lanes], (nk, LANES)).astype(BF16)

    def gates(row0, a_scr, g_scr):
        for lt in range(tm // LANES):
            lanes = slice(lt * LANES, (lt + 1) * LANES)
            w = [zero] * rph
            for h in range(PEER_HEADS):
                rk = _unpack_pairs(rank_ref[h, :, lanes])
                ev = _unpack_pairs(e2_ref[h, :, lanes])
                for r in range(rph):
                    cnt = packed_row(n_ref, h, row0 + r, lanes)
                    coef = packed_row(b_ref, h, row0 + r, lanes)
                    w[r] = w[r] + jnp.where(rk < cnt, ev * coef, zero)
            for r in range(rph):
                rows = slice(r * nk, (r + 1) * nk)
                a = a_scr[rows, lanes]
                gel = 0.5 * a * (1.0 + lax.erf(a * 0.7071067811865476))
                g_scr[rows, lanes] = w[r] * gel.astype(BF16)

    xb = xb_scr[...]
    acc_scr[...] += _dot(vta_ref[...], g1_scr[...])
    a0_scr[...] = _dot_nt(u_ref[0:nbh, :], xb)
    gates(0, a0_scr, g0_scr)
    a1_scr[...] = _dot_nt(u_ref[nbh:2 * nbh, :], xb)
    acc_scr[...] += _dot(vtb_ref[:, 0:nbh], g0_scr[...])
    gates(rph, a1_scr, g1_scr)

    @pl.when(s == pl.num_programs(1) - 1)
    def _():
        pe = acc_scr[...] + _dot(vtb_ref[:, nbh:2 * nbh], g1_scr[...])
        x2 = x1_ref[...] + mod_ref[0, 5:6, :] * pe.T
        y_ref[...] = _rms_rows(x2, fg_ref[...])


def _peer(h2, x1, mod3, final_g, rank2, e2, cnt, bcoef, u_bf, vt_bf, n_prompt_tokens, sample_len):
    t = h2.shape[0]
    tm = 512
    rph = PEER_ROWS_PER_HALF
    nbh = rph * PEER_NKEYS
    n_exp = u_bf.shape[0]
    row = functools.partial(_mod_row, tile_tokens=tm, n_prompt_tokens=n_prompt_tokens,
                            sample_len=sample_len)
    route = pl.BlockSpec((PEER_HEADS, PEER_NKEYS // 2, tm), lambda i, s: (0, 0, i))
    route_rows = pl.BlockSpec((PEER_HEADS, 2 * rph, tm), lambda i, s: (0, s, i))
    tok = pl.BlockSpec((tm, D_MODEL), lambda i, s: (i, 0))
    return pl.pallas_call(
        _peer_kernel,
        grid=(t // tm, n_exp // (2 * nbh)),
        in_specs=[tok, tok,
                  pl.BlockSpec((1, 6, D_MODEL), lambda i, s: (row(i), 0, 0)),
                  pl.BlockSpec((1, D_MODEL), lambda i, s: (0, 0)),
                  route, route, route_rows, route_rows,
                  pl.BlockSpec((2 * nbh, D_MODEL), lambda i, s: (s, 0)),
                  pl.BlockSpec((D_MODEL, nbh), lambda i, s: (0, jnp.maximum(2 * s - 1, 0))),
                  pl.BlockSpec((D_MODEL, 2 * nbh), lambda i, s: (0, s))],
        out_specs=tok,
        out_shape=jax.ShapeDtypeStruct((t, D_MODEL), F32),
        scratch_shapes=[pltpu.VMEM((D_MODEL, tm), F32),
                        pltpu.VMEM((tm, D_MODEL), BF16),
                        pltpu.VMEM((nbh, tm), F32),
                        pltpu.VMEM((nbh, tm), F32),
                        pltpu.VMEM((nbh, tm), BF16),
                        pltpu.VMEM((nbh, tm), BF16)],
        compiler_params=pltpu.CompilerParams(dimension_semantics=("parallel", "arbitrary"),
                                             vmem_limit_bytes=VMEM_LIMIT),
        name="peer_experts",
    )(h2, x1, mod3, final_g, rank2, e2, cnt, bcoef, u_bf, vt_bf, vt_bf)


def kernel(x_prompt, x_sample, state_hgrn, c, c_ctx, norm_g, w_ada, b_ada, w_in, hgrn_lb_logits, hgrn_gnorm_g, w_hgrn_out, conv_w, conv_b, conv_ln_g, conv_ln_b, w_conv_out, w_mix_out, peer_wq, peer_keys, peer_u, peer_v, final_g):
    n_prompt, prompt_len, d = x_prompt.shape
    n_sample, sample_len, _ = x_sample.shape
    depth = w_in.shape[0]
    assert d == D_MODEL and depth == 1 and prompt_len == MIX_TILE
    assert n_sample + 1 <= 8 and sample_len % MIX_TILE == 0
    n_prompt_tokens = n_prompt * prompt_len
    assert n_prompt_tokens % sample_len == 0
    layer = 0
    hw = HEADS * HEAD_DIM

    x_all = jnp.concatenate([x_prompt.reshape(n_prompt_tokens, d),
                             x_sample.reshape(n_sample * sample_len, d)], axis=0)

    cond8 = jnp.zeros((8, d), F32).at[0].set(c_ctx).at[1:1 + n_sample].set(c)
    mod3 = _ada(cond8, w_ada[layer], b_ada[layer]).reshape(8, 6, d)

    w = w_in[layer]
    w_perm = jnp.concatenate([w[:, hw:3 * hw], w[:, :hw], w[:, 3 * hw:]], axis=1).astype(BF16)
    gates, rest = _inproj(x_all, mod3, norm_g[layer, 0:1], w_perm, n_prompt_tokens, sample_len)

    a_np, m_np = _hgrn_consts()
    a_c = jnp.asarray(a_np, BF16)
    m_c = jnp.asarray(m_np, F32)
    gnorm = hgrn_gnorm_g[layer].reshape(1, HEAD_DIM)
    oa0 = jnp.zeros((x_all.shape[0], d), BF16)
    oa1, new_state = _hgrn(gates, rest, 0, n_prompt, prompt_len, hgrn_lb_logits, gnorm, a_c, m_c,
                           None, True, oa0)
    (oa,) = _hgrn(gates, rest, n_prompt_tokens // sample_len, n_sample, sample_len, hgrn_lb_logits,
                  gnorm, a_c, m_c, state_hgrn, False, oa1)

    row = lambda v: v.reshape(1, d)
    x1, h2 = _mix(x_all, oa, rest, mod3, norm_g[layer, 1:2], conv_w[layer], row(conv_b[layer]),
                  row(conv_ln_g[layer]), row(conv_ln_b[layer]), w_conv_out[layer].astype(BF16),
                  w_hgrn_out[layer].astype(BF16), w_mix_out[layer].astype(BF16),
                  n_prompt_tokens, sample_len)

    wq_t = peer_wq[layer].T
    wq_hi = wq_t.astype(BF16)
    wq_lo = (wq_t - wq_hi.astype(F32)).astype(BF16)
    rank2, e2, cnt, bcoef = _topk(h2, wq_hi, wq_lo, peer_keys[layer])

    y = _peer(h2, x1, mod3, row(final_g), rank2, e2, cnt, bcoef,
              peer_u[layer].astype(BF16), peer_v[layer].T.astype(BF16), n_prompt_tokens, sample_len)

    y_prompt = y[:n_prompt_tokens].reshape(n_prompt, prompt_len, d)
    y_sample = y[n_prompt_tokens:].reshape(n_sample, sample_len, d)
    return y_prompt, y_sample, new_state
```

```python
import functools

import numpy as np
import jax
import jax.numpy as jnp
from jax import lax
from jax.experimental import pallas as pl
from jax.experimental.pallas import tpu as pltpu

D_MODEL = 1024
LANES = 128
HEADS = 8
HEAD_DIM = 128
HEADS_PER_STEP = 2
CONV_K = 31
CONV_HALO = 16
CONV_ROWS = 64
PEER_HEADS = 8
PEER_NKEYS = 128
PEER_TOPK = 16
EPS = 1e-6
CHUNK = 128
N_LEVELS = 7
MIX_TILE = 256
N_GATE_BLOCKS = 2
VMEM_LIMIT = 56 * 1024 * 1024

F32 = jnp.float32
BF16 = jnp.bfloat16


def _dot(a, b):
    return jnp.dot(a, b, preferred_element_type=F32)


def _dot_nt(a, b):
    return lax.dot_general(a, b, (((1,), (1,)), ((), ())), preferred_element_type=F32)


def _dot_tn(a, b):
    return lax.dot_general(a, b, (((0,), (0,)), ((), ())), preferred_element_type=F32)


def _split2(x):
    hi = x.astype(BF16)
    lo = (x - hi.astype(F32)).astype(BF16)
    return hi, lo


def _pack_pairs(x):
    return pltpu.bitcast(x.astype(BF16), jnp.uint32)


def _unpack_pairs(x):
    return pltpu.bitcast(x, BF16)


def _sigmoid(x):
    return 1.0 / (1.0 + jnp.exp(-x))


def _rms_rows(x, g):
    ms = jnp.mean(x * x, axis=-1, keepdims=True)
    return x * lax.rsqrt(ms + EPS) * g


def _mod_row(tile, tile_tokens, n_prompt_tokens, sample_len):
    npt = n_prompt_tokens // tile_tokens
    tps = sample_len // tile_tokens
    return jnp.where(tile < npt, 0, 1 + (tile - npt) // tps)


def _ada_kernel(cond_ref, w_ref, b_ref, o_ref):
    c = cond_ref[...]
    s = c * _sigmoid(c)
    o_ref[...] = jnp.dot(s, w_ref[...], preferred_element_type=F32,
                         precision=lax.Precision.HIGHEST) + b_ref[...]


def _ada(cond8, w_ada, b_ada):
    n_out = w_ada.shape[1]
    bn = D_MODEL
    return pl.pallas_call(
        _ada_kernel,
        grid=(n_out // bn,),
        in_specs=[pl.BlockSpec((8, D_MODEL), lambda j: (0, 0)),
                  pl.BlockSpec((D_MODEL, bn), lambda j: (0, j)),
                  pl.BlockSpec((1, bn), lambda j: (0, j))],
        out_specs=pl.BlockSpec((8, bn), lambda j: (0, j)),
        out_shape=jax.ShapeDtypeStruct((8, n_out), F32),
        compiler_params=pltpu.CompilerParams(dimension_semantics=("parallel",),
                                             vmem_limit_bytes=VMEM_LIMIT),
        name="ada",
    )(cond8, w_ada, b_ada.reshape(1, n_out))


def _split_token_specs(tile, n_prompt_tiles):
    ctx = pl.BlockSpec((tile, D_MODEL), lambda i, *_: (jnp.minimum(i, n_prompt_tiles - 1), 0))
    lat = pl.BlockSpec((tile, D_MODEL), lambda i, *_: (jnp.maximum(i - n_prompt_tiles, 0), 0))
    return ctx, lat


def _inproj_kernel(xp_ref, xs_ref, mod_ref, ng_ref, w_ref, gate_ref, rest_ref, h_scr, *, n_prompt_tiles):
    i = pl.program_id(0)
    j = pl.program_id(1)

    def normalise(x_ref):
        y = _rms_rows(x_ref[...], ng_ref[...])
        h = y * (1.0 + mod_ref[0, 1:2, :]) + mod_ref[0, 0:1, :]
        h_scr[...] = h.astype(BF16)

    @pl.when(jnp.logical_and(j == 0, i < n_prompt_tiles))
    def _():
        normalise(xp_ref)

    @pl.when(jnp.logical_and(j == 0, i >= n_prompt_tiles))
    def _():
        normalise(xs_ref)

    @pl.when(j < N_GATE_BLOCKS)
    def _():
        gate_ref[...] = _dot(h_scr[...], w_ref[...])

    @pl.when(j >= N_GATE_BLOCKS)
    def _():
        rest_ref[...] = _dot(h_scr[...], w_ref[...]).astype(BF16)


def _inproj(x_ctx, x_lat, mod3, ng0, w_perm_bf, sample_len):
    n_prompt_tokens = x_ctx.shape[0]
    t = n_prompt_tokens + x_lat.shape[0]
    n_out = w_perm_bf.shape[1]
    tm, tn = 1024, D_MODEL
    nj = n_out // tn
    row = functools.partial(_mod_row, tile_tokens=tm, n_prompt_tokens=n_prompt_tokens,
                            sample_len=sample_len)
    ctx, lat = _split_token_specs(tm, n_prompt_tokens // tm)
    return pl.pallas_call(
        functools.partial(_inproj_kernel, n_prompt_tiles=n_prompt_tokens // tm),
        grid=(t // tm, nj),
        in_specs=[ctx, lat,
                  pl.BlockSpec((1, 6, D_MODEL), lambda i, j: (row(i), 0, 0)),
                  pl.BlockSpec((1, D_MODEL), lambda i, j: (0, 0)),
                  pl.BlockSpec((D_MODEL, tn), lambda i, j: (0, j))],
        out_specs=[pl.BlockSpec((tm, tn), lambda i, j: (i, jnp.minimum(j, N_GATE_BLOCKS - 1))),
                   pl.BlockSpec((tm, tn), lambda i, j: (i, jnp.maximum(j - N_GATE_BLOCKS, 0)))],
        out_shape=[jax.ShapeDtypeStruct((t, N_GATE_BLOCKS * tn), F32),
                   jax.ShapeDtypeStruct((t, n_out - N_GATE_BLOCKS * tn), BF16)],
        scratch_shapes=[pltpu.VMEM((tm, D_MODEL), BF16)],
        compiler_params=pltpu.CompilerParams(dimension_semantics=("parallel", "arbitrary"),
                                             vmem_limit_bytes=VMEM_LIMIT),
        name="inproj",
    )(x_ctx, x_lat, mod3, ng0, w_perm_bf)


def _hgrn_consts():
    c = CHUNK
    t = np.arange(c)[:, None]
    r = np.arange(c)[None, :]
    a_list, m_list = [], []
    for lvl in range(N_LEVELS):
        b = 1 << lvl
        t_base = (t // (2 * b)) * (2 * b)
        bound = t_base + b - 1
        t_right = (t - t_base) >= b
        a_right = (r > bound) & (r <= t)
        a_left = (r > t) & (r <= bound)
        a_list.append(np.where(t_right, a_right, a_left))
        r_base = (r // (2 * b)) * (2 * b)
        m_list.append((t_base == r_base) & t_right & ((r - r_base) < b))
    a_list.append(r <= t)
    a_list.append(r > t)
    m_list.append(t == r)
    a_f = np.stack(a_list).astype(np.float32)
    m_f = np.stack(m_list).astype(np.float32)
    a_b = a_f[:, ::-1, ::-1]
    m_b = m_f[:, ::-1, ::-1]
    a = np.stack([a_f, a_b]).reshape(2, (N_LEVELS + 2) * c, c)
    m = np.stack([m_f, m_b])
    return a, m


def _hgrn_kernel(*refs, n_chunks, has_s0, emit_state):
    q_ref, ff_ref, fb_ref, iv_ref, og_ref, lbl_ref, gn_ref, a_ref, m_ref = refs[:9]
    pos = 9
    s0_ref = None
    if has_s0:
        s0_ref = refs[pos]
        pos += 1
    o_ref = refs[pos]
    pos += 1
    st_ref = None
    if emit_state:
        st_ref = refs[pos]
        pos += 1
    of_scr, ob_scr, s_scr, x_scr, k_scr = refs[pos:pos + 5]
    c = CHUNK
    hd = HEAD_DIM
    nh = HEADS_PER_STEP

    logits = lbl_ref[...]
    e = jnp.exp(logits - jnp.max(logits, axis=0, keepdims=True))
    lb = e[0] / jnp.sum(e, axis=0)

    for d in range(2):
        for hh in range(nh):
            if has_s0:
                s_scr[d, hh] = s0_ref[0, 0, d, hh].T
            else:
                s_scr[d, hh] = jnp.zeros((hd, hd), F32)

    def chunk_rows(d, step):
        ci = step if d == 0 else n_chunks - 1 - step
        return pl.ds(pl.multiple_of(ci * c, c), c)

    def decay_stage(d, step, slot):
        rows = chunk_rows(d, step)
        z = (ff_ref if d == 0 else fb_ref)[rows, :]
        lbd = lb[d:d + 1, :]
        f = lbd + (1.0 - lbd) * _sigmoid(z)
        g1, g2 = _split2(jnp.log(f))
        a = a_ref[d]
        x_scr[d, slot] = jnp.exp(_dot(a, g1) + _dot(a, g2))
        k_scr[d, slot] = (1.0 - lbd) * _sigmoid(-z)

    def matmul_stage(d, step, slot):
        rows = chunk_rows(d, step)
        qv = q_ref[rows, :].astype(F32)
        qs = qv * _sigmoid(qv)
        k = k_scr[d, slot]
        v = iv_ref[rows, :]

        def x_blk(i):
            return x_scr[d, slot, i * c:(i + 1) * c, :]

        x_cum = x_blk(N_LEVELS)
        qb = qs.astype(BF16)
        kb = k.astype(BF16)
        q_cum = (qs * x_cum).astype(BF16)
        k_hi, k_lo = _split2(k * x_blk(N_LEVELS + 1))
        q_lvl = [(qs * x_blk(lvl)).astype(BF16) for lvl in range(N_LEVELS)]
        k_lvl = [(k * x_blk(lvl)).astype(BF16) for lvl in range(N_LEVELS)]
        for hh in range(nh):
            cols = slice(hh * hd, (hh + 1) * hd)
            att = m_ref[d, N_LEVELS] * _dot_nt(qb[:, cols], kb[:, cols])
            for lvl in range(N_LEVELS):
                att = att + m_ref[d, lvl] * _dot_nt(q_lvl[lvl][:, cols], k_lvl[lvl][:, cols])
            st = s_scr[d, hh]
            vh = v[:, cols]
            o = _dot(att.astype(BF16), vh) + _dot_nt(q_cum[:, cols], st.astype(BF16))
            (of_scr if d == 0 else ob_scr)[rows, cols] = o
            total = x_cum[c - 1:c, cols] if d == 0 else x_cum[0:1, cols]
            s_scr[d, hh] = total * st + (_dot_tn(vh, k_hi[:, cols]) + _dot_tn(vh, k_lo[:, cols]))

    def pair(step0, last):
        for par in range(2):
            step = step0 + par
            for d in range(2):
                matmul_stage(d, step, par)
            if not (last and par == 1):
                for d in range(2):
                    decay_stage(d, step + 1, 1 - par)

    for d in range(2):
        decay_stage(d, 0, 0)

    def body(cp, carry):
        pair(2 * cp, False)
        return carry

    if n_chunks > 2:
        lax.fori_loop(0, n_chunks // 2 - 1, body, 0)
    pair(n_chunks - 2, True)

    og = og_ref[...].astype(F32)
    gate = og * _sigmoid(og)
    for hh in range(nh):
        cols = slice(hh * hd, (hh + 1) * hd)
        o = of_scr[:, cols] + ob_scr[:, cols]
        o_ref[:, cols] = (_rms_rows(o, gn_ref[...]) * gate[:, cols]).astype(BF16)
    if emit_state:
        for d in range(2):
            for hh in range(nh):
                st_ref[0, 0, d, hh] = s_scr[d, hh].T


def _hgrn(gates, rest, row_block0, n_seq, seq_len, lb_logits, gnorm, a_c, m_c, s0, emit_state, oa_prev):
    t = gates.shape[0]
    nl = lb_logits.shape[0]
    has_s0 = s0 is not None
    nh = HEADS_PER_STEP
    w = nh * HEAD_DIM
    per = D_MODEL // w

    def col(cb):
        return pl.BlockSpec((seq_len, w), lambda b, h: (row_block0 + b, cb * per + h))

    in_specs = [col(0), col(0), col(1), col(1), col(2),
                pl.BlockSpec((nl, 2, w), lambda b, h: (0, 0, h)),
                pl.BlockSpec((1, HEAD_DIM), lambda b, h: (0, 0)),
                pl.BlockSpec(a_c.shape, lambda b, h: (0, 0, 0)),
                pl.BlockSpec(m_c.shape, lambda b, h: (0, 0, 0, 0))]
    args = [rest, gates, gates, rest, rest, lb_logits, gnorm, a_c, m_c]
    state_block = (1, 1, 2, nh, HEAD_DIM, HEAD_DIM)
    if has_s0:
        in_specs.append(pl.BlockSpec(state_block, lambda b, h: (b, 0, 0, h, 0, 0)))
        args.append(s0)
    in_specs.append(pl.BlockSpec(memory_space=pl.ANY))
    args.append(oa_prev)
    out_specs = [pl.BlockSpec((seq_len, w), lambda b, h: (row_block0 + b, h))]
    out_shape = [jax.ShapeDtypeStruct((t, D_MODEL), BF16)]
    if emit_state:
        out_specs.append(pl.BlockSpec(state_block, lambda b, h: (b, 0, 0, h, 0, 0)))
        out_shape.append(jax.ShapeDtypeStruct((n_seq, 1, 2, HEADS, HEAD_DIM, HEAD_DIM), F32))

    def kern(*refs):
        n_in = len(args)
        refs = refs[:n_in - 1] + refs[n_in:]
        _hgrn_kernel(*refs, n_chunks=seq_len // CHUNK, has_s0=has_s0, emit_state=emit_state)

    outs = pl.pallas_call(
        kern,
        grid=(n_seq, HEADS // nh),
        in_specs=in_specs,
        out_specs=out_specs,
        out_shape=out_shape,
        scratch_shapes=[pltpu.VMEM((seq_len, w), F32),
                        pltpu.VMEM((seq_len, w), F32),
                        pltpu.VMEM((2, nh, HEAD_DIM, HEAD_DIM), F32),
                        pltpu.VMEM((2, 2, (N_LEVELS + 2) * CHUNK, w), F32),
                        pltpu.VMEM((2, 2, CHUNK, w), F32)],
        input_output_aliases={len(args) - 1: 0},
        compiler_params=pltpu.CompilerParams(dimension_semantics=("parallel", "parallel"),
                                             vmem_limit_bytes=VMEM_LIMIT),
        name="hgrn_ctx" if emit_state else "hgrn_lat",
    )(*args)
    return outs


def _mix_kernel(xp_ref, xs_ref, oa_ref, ga_ref, gb_ref, gta_ref, gtb_ref, gap_ref, gan_ref, gbp_ref, gbn_ref,
                mod_ref, ng_ref, cw_ref, cb_ref, lng_ref, lnb_ref, wco_ref, who_ref, wmo_ref,
                x1_ref, h2_ref, u_scr, conv_scr, *, n_prompt_tiles, tiles_per_sample):
    tm = MIX_TILE
    hl = CONV_HALO
    i = pl.program_id(0)
    j = lax.rem(jnp.maximum(i - n_prompt_tiles, 0), tiles_per_sample)
    is_prompt = i < n_prompt_tiles
    keep_prev = jnp.where(jnp.logical_or(is_prompt, j == 0), 0.0, 1.0)
    keep_next = jnp.where(jnp.logical_or(is_prompt, j == tiles_per_sample - 1), 0.0, 1.0)

    def glu(a_ref, b_ref):
        return a_ref[...].astype(F32) * _sigmoid(b_ref[...].astype(F32))

    u_scr[0:hl, :] = glu(gap_ref, gbp_ref) * keep_prev
    u_scr[hl:hl + tm, :] = glu(ga_ref, gb_ref)
    u_scr[hl + tm:hl + tm + hl, :] = glu(gan_ref, gbn_ref) * keep_next

    off0 = hl - CONV_K // 2
    rb = CONV_ROWS
    span = rb + 8 * (-(-(CONV_K + off0) // 8))
    for cblk in range(D_MODEL // LANES):
        cols = slice(cblk * LANES, (cblk + 1) * LANES)
        wv = cw_ref[:, cols]
        taps = [jnp.broadcast_to(wv[k:k + 1, :], (rb, LANES)) for k in range(CONV_K)]
        bias = jnp.broadcast_to(cb_ref[:, cols], (rb, LANES))
        for r0 in range(0, tm, rb):
            ext = u_scr[r0:r0 + span, cols]
            acc = bias
            for s in range(8):
                us = ext if s == 0 else pltpu.roll(ext, span - s, axis=0)
                for q in range(span // 8):
                    k = 8 * q + s - off0
                    if 0 <= k < CONV_K:
                        acc = acc + taps[k] * us[8 * q:8 * q + rb, :]
            conv_scr[r0:r0 + rb, cols] = acc

    acc = conv_scr[...]
    mu = jnp.mean(acc, axis=-1, keepdims=True)
    cen = acc - mu
    var = jnp.mean(cen * cen, axis=-1, keepdims=True)
    y = cen * lax.rsqrt(var + EPS) * lng_ref[...] + lnb_ref[...]
    u2 = y * _sigmoid(y)
    y_b = _dot(u2.astype(BF16), wco_ref[...])
    y_a = _dot(oa_ref[...], who_ref[...])
    mix = _sigmoid(gta_ref[...].astype(F32)) * y_a + _sigmoid(gtb_ref[...].astype(F32)) * y_b
    branch = mod_ref[0, 2:3, :] * _dot(mix.astype(BF16), wmo_ref[...])

    def residual(x_ref):
        x1 = x_ref[...] + branch
        x1_ref[...] = x1
        h2_ref[...] = _rms_rows(x1, ng_ref[...]) * (1.0 + mod_ref[0, 4:5, :]) + mod_ref[0, 3:4, :]

    @pl.when(is_prompt)
    def _():
        residual(xp_ref)

    @pl.when(jnp.logical_not(is_prompt))
    def _():
        residual(xs_ref)


def _mix(x_ctx, x_lat, oa, rest, mod3, ng1, conv_w, conv_b, ln_g, ln_b, wco, who, wmo,
         sample_len):
    n_prompt_tokens = x_ctx.shape[0]
    t = n_prompt_tokens + x_lat.shape[0]
    tm = MIX_TILE
    hl = CONV_HALO
    r = tm // hl
    n_halo_blocks = t // hl
    col0 = 3
    row = functools.partial(_mod_row, tile_tokens=tm, n_prompt_tokens=n_prompt_tokens,
                            sample_len=sample_len)

    def tile(cb):
        return pl.BlockSpec((tm, D_MODEL), lambda i: (i, cb))

    def prev(cb):
        return pl.BlockSpec((hl, D_MODEL), lambda i: (jnp.maximum(i * r - 1, 0), cb))

    def nxt(cb):
        return pl.BlockSpec((hl, D_MODEL), lambda i: (jnp.minimum((i + 1) * r, n_halo_blocks - 1), cb))

    def full(shape):
        return pl.BlockSpec(shape, lambda i: (0,) * len(shape))

    ctx, lat = _split_token_specs(tm, n_prompt_tokens // tm)
    kern = functools.partial(_mix_kernel, n_prompt_tiles=n_prompt_tokens // tm,
                             tiles_per_sample=sample_len // tm)
    return pl.pallas_call(
        kern,
        grid=(t // tm,),
        in_specs=[ctx, lat, tile(0), tile(col0), tile(col0 + 1), tile(col0 + 2), tile(col0 + 3),
                  prev(col0), nxt(col0), prev(col0 + 1), nxt(col0 + 1),
                  pl.BlockSpec((1, 6, D_MODEL), lambda i: (row(i), 0, 0)),
                  full((1, D_MODEL)), full((CONV_K, D_MODEL)), full((1, D_MODEL)),
                  full((1, D_MODEL)), full((1, D_MODEL)),
                  full((D_MODEL, D_MODEL)), full((D_MODEL, D_MODEL)), full((D_MODEL, D_MODEL))],
        out_specs=[tile(0), tile(0)],
        out_shape=[jax.ShapeDtypeStruct((t, D_MODEL), F32), jax.ShapeDtypeStruct((t, D_MODEL), F32)],
        scratch_shapes=[pltpu.VMEM((tm + 2 * hl, D_MODEL), F32),
                        pltpu.VMEM((tm, D_MODEL), F32)],
        compiler_params=pltpu.CompilerParams(dimension_semantics=("parallel",),
                                             vmem_limit_bytes=VMEM_LIMIT),
        name="mix",
    )(x_ctx, x_lat, oa, rest, rest, rest, rest, rest, rest, rest, rest, mod3, ng1, conv_w, conv_b, ln_g, ln_b,
      wco, who, wmo)


def _cand_rows():
    return [(a, PEER_TOPK // (a + 1)) for a in range(PEER_TOPK)]


N_CAND = sum(nb for _, nb in _cand_rows())
N_CAND_PAD = -(-N_CAND // 8) * 8


def _topk_kernel(h2_ref, wqh_ref, wql_ref, keys_ref, rank_ref, e2_ref, n_ref, b_ref,
                 q_scr, v_scr, cand_scr):
    nk = PEER_NKEYS
    tm = h2_ref.shape[0]
    hh, hl = _split2(h2_ref[...])
    wqh = wqh_ref[...]
    q_scr[...] = _dot_nt(wqh, hh) + _dot_nt(wqh, hl) + _dot_nt(wql_ref[...], hh)
    neg = -jnp.inf

    def head(h, carry):
        kparts = [_split2(keys_ref[h, p]) for p in range(2)]
        for lt in range(tm // LANES):
            lanes = slice(lt * LANES, (lt + 1) * LANES)
            s = []
            for p in range(2):
                kh, kl = kparts[p]
                qh, ql = _split2(q_scr[pl.ds(pl.multiple_of(h * 2 * nk + p * nk, nk), nk), lanes])
                sp = _dot(kh, qh) + _dot(kh, ql) + _dot(kl, qh)
                s.append(sp)
                cur = sp
                rank = jnp.full(sp.shape, float(PEER_TOPK), F32)
                for a in range(PEER_TOPK):
                    m = jnp.max(cur, axis=0, keepdims=True)
                    hit = cur == m
                    if p == 1:
                        rank = jnp.where(hit, float(a), rank)
                    cur = jnp.where(hit, neg, cur)
                    v_scr[p, a:a + 1, :] = m
                if p == 1:
                    rank_ref[h, :, lanes] = _pack_pairs(rank)
            s1, s2 = s
            v1 = v_scr[0]
            v2 = v_scr[1]
            off = 0
            for a, nb in _cand_rows():
                cand_scr[off:off + nb, :] = v1[a:a + 1, :] + v2[0:nb, :]
                off += nb
            if N_CAND_PAD > N_CAND:
                cand_scr[N_CAND:N_CAND_PAD, :] = jnp.full((N_CAND_PAD - N_CAND, LANES), neg, F32)
            cand = cand_scr[...]
            cur = cand
            tau = None
            for a in range(PEER_TOPK):
                tau = jnp.max(cur, axis=0, keepdims=True)
                if a < PEER_TOPK - 1:
                    cur = jnp.where(cur == tau, neg, cur)
            top = v1[0:1, :] + v2[0:1, :]
            zsum = jnp.sum(jnp.where(cand >= tau, jnp.exp(cand - top), 0.0), axis=0, keepdims=True)
            cnt = jnp.zeros(s1.shape, F32)
            for b in range(PEER_TOPK):
                cnt = cnt + jnp.where(s1 + v2[b:b + 1, :] >= tau, 1.0, 0.0)
            n_ref[h, :, lanes] = cnt
            b_ref[h, :, lanes] = jnp.exp(s1 - v1[0:1, :]) / zsum
            e2_ref[h, :, lanes] = _pack_pairs(jnp.exp(s2 - v2[0:1, :]))
        return carry

    lax.fori_loop(0, PEER_HEADS, head, 0)


def _topk(h2, wq_t_hi, wq_t_lo, keys):
    t = h2.shape[0]
    tm = 256
    nq = wq_t_hi.shape[0]
    route = pl.BlockSpec((PEER_HEADS, PEER_NKEYS, tm), lambda i: (0, 0, i))
    packed = pl.BlockSpec((PEER_HEADS, PEER_NKEYS // 2, tm), lambda i: (0, 0, i))

    def route_shape(dt):
        return jax.ShapeDtypeStruct((PEER_HEADS, PEER_NKEYS, t), dt)

    packed_shape = jax.ShapeDtypeStruct((PEER_HEADS, PEER_NKEYS // 2, t), jnp.uint32)

    return pl.pallas_call(
        _topk_kernel,
        grid=(t // tm,),
        in_specs=[pl.BlockSpec((tm, D_MODEL), lambda i: (i, 0)),
                  pl.BlockSpec((nq, D_MODEL), lambda i: (0, 0)),
                  pl.BlockSpec((nq, D_MODEL), lambda i: (0, 0)),
                  pl.BlockSpec(keys.shape, lambda i: (0, 0, 0, 0))],
        out_specs=[packed, packed, route, route],
        out_shape=[packed_shape, packed_shape, route_shape(F32), route_shape(F32)],
        scratch_shapes=[pltpu.VMEM((nq, tm), F32),
                        pltpu.VMEM((2, PEER_TOPK, LANES), F32),
                        pltpu.VMEM((N_CAND_PAD, LANES), F32)],
        compiler_params=pltpu.CompilerParams(dimension_semantics=("parallel",),
                                             vmem_limit_bytes=VMEM_LIMIT),
        name="peer_topk",
    )(h2, wq_t_hi, wq_t_lo, keys)


PEER_ROWS_PER_HALF = 4


def _peer_kernel(h2_ref, x1_ref, mod_ref, fg_ref, rank_ref, e2_ref, n_ref, b_ref, u_ref, vta_ref, vtb_ref,
                 yp_ref, ys_ref, acc_scr, xb_scr, a0_scr, a1_scr, g0_scr, g1_scr, *, n_prompt_tiles):
    nk = PEER_NKEYS
    rph = PEER_ROWS_PER_HALF
    nbh = rph * nk
    tm = h2_ref.shape[0]
    s = pl.program_id(1)

    @pl.when(s == 0)
    def _():
        acc_scr[...] = jnp.zeros(acc_scr.shape, F32)
        xb_scr[...] = h2_ref[...].astype(BF16)
        g1_scr[...] = jnp.zeros(g1_scr.shape, BF16)

    zero = jnp.zeros((nk, LANES), BF16)

    def packed_row(ref, h, ii, lanes):
        return jnp.broadcast_to(ref[h, ii:ii + 1, lanes], (nk, LANES)).astype(BF16)

    def gates(row0, a_scr, g_scr):
        for lt in range(tm // LANES):
            lanes = slice(lt * LANES, (lt + 1) * LANES)
            w = [zero] * rph
            for h in range(PEER_HEADS):
                rk = _unpack_pairs(rank_ref[h, :, lanes])
                ev = _unpack_pairs(e2_ref[h, :, lanes])
                for r in range(rph):
                    cnt = packed_row(n_ref, h, row0 + r, lanes)
                    coef = packed_row(b_ref, h, row0 + r, lanes)
                    w[r] = w[r] + jnp.where(rk < cnt, ev * coef, zero)
            for r in range(rph):
                rows = slice(r * nk, (r + 1) * nk)
                a = a_scr[rows, lanes]
                gel = 0.5 * a * (1.0 + lax.erf(a * 0.7071067811865476))
                g_scr[rows, lanes] = w[r] * gel.astype(BF16)

    xb = xb_scr[...]
    acc_scr[...] += _dot(vta_ref[...], g1_scr[...])
    a0_scr[...] = _dot_nt(u_ref[0:nbh, :], xb)
    gates(0, a0_scr, g0_scr)
    a1_scr[...] = _dot_nt(u_ref[nbh:2 * nbh, :], xb)
    acc_scr[...] += _dot(vtb_ref[:, 0:nbh], g0_scr[...])
    gates(rph, a1_scr, g1_scr)

    def finish(y_ref):
        pe = acc_scr[...] + _dot(vtb_ref[:, nbh:2 * nbh], g1_scr[...])
        x2 = x1_ref[...] + mod_ref[0, 5:6, :] * pe.T
        y_ref[...] = _rms_rows(x2, fg_ref[...])

    last = s == pl.num_programs(1) - 1
    is_prompt = pl.program_id(0) < n_prompt_tiles

    @pl.when(jnp.logical_and(last, is_prompt))
    def _():
        finish(yp_ref)

    @pl.when(jnp.logical_and(last, jnp.logical_not(is_prompt)))
    def _():
        finish(ys_ref)


def _peer(h2, x1, mod3, final_g, rank2, e2, cnt, bcoef, u_bf, vt_bf, n_prompt_tokens, sample_len):
    t = h2.shape[0]
    tm = 512
    rph = PEER_ROWS_PER_HALF
    nbh = rph * PEER_NKEYS
    n_exp = u_bf.shape[0]
    row = functools.partial(_mod_row, tile_tokens=tm, n_prompt_tokens=n_prompt_tokens,
                            sample_len=sample_len)
    route = pl.BlockSpec((PEER_HEADS, PEER_NKEYS // 2, tm), lambda i, s: (0, 0, i))
    route_rows = pl.BlockSpec((PEER_HEADS, 2 * rph, tm), lambda i, s: (0, s, i))
    tok = pl.BlockSpec((tm, D_MODEL), lambda i, s: (i, 0))
    return pl.pallas_call(
        functools.partial(_peer_kernel, n_prompt_tiles=n_prompt_tokens // tm),
        grid=(t // tm, n_exp // (2 * nbh)),
        in_specs=[tok, tok,
                  pl.BlockSpec((1, 6, D_MODEL), lambda i, s: (row(i), 0, 0)),
                  pl.BlockSpec((1, D_MODEL), lambda i, s: (0, 0)),
                  route, route, route_rows, route_rows,
                  pl.BlockSpec((2 * nbh, D_MODEL), lambda i, s: (s, 0)),
                  pl.BlockSpec((D_MODEL, nbh), lambda i, s: (0, jnp.maximum(2 * s - 1, 0))),
                  pl.BlockSpec((D_MODEL, 2 * nbh), lambda i, s: (0, s))],
        out_specs=list(_split_token_specs(tm, n_prompt_tokens // tm)),
        out_shape=[jax.ShapeDtypeStruct((n_prompt_tokens, D_MODEL), F32),
                   jax.ShapeDtypeStruct((t - n_prompt_tokens, D_MODEL), F32)],
        scratch_shapes=[pltpu.VMEM((D_MODEL, tm), F32),
                        pltpu.VMEM((tm, D_MODEL), BF16),
                        pltpu.VMEM((nbh, tm), F32),
                        pltpu.VMEM((nbh, tm), F32),
                        pltpu.VMEM((nbh, tm), BF16),
                        pltpu.VMEM((nbh, tm), BF16)],
        compiler_params=pltpu.CompilerParams(dimension_semantics=("arbitrary", "arbitrary"),
                                             vmem_limit_bytes=VMEM_LIMIT),
        name="peer_experts",
    )(h2, x1, mod3, final_g, rank2, e2, cnt, bcoef, u_bf, vt_bf, vt_bf)


def kernel(x_prompt, x_sample, state_hgrn, c, c_ctx, norm_g, w_ada, b_ada, w_in, hgrn_lb_logits, hgrn_gnorm_g, w_hgrn_out, conv_w, conv_b, conv_ln_g, conv_ln_b, w_conv_out, w_mix_out, peer_wq, peer_keys, peer_u, peer_v, final_g):
    n_prompt, prompt_len, d = x_prompt.shape
    n_sample, sample_len, _ = x_sample.shape
    depth = w_in.shape[0]
    assert d == D_MODEL and depth == 1 and prompt_len == MIX_TILE
    assert n_sample + 1 <= 8 and sample_len % MIX_TILE == 0
    n_prompt_tokens = n_prompt * prompt_len
    assert n_prompt_tokens % sample_len == 0
    layer = 0
    hw = HEADS * HEAD_DIM

    x_ctx = x_prompt.reshape(n_prompt_tokens, d)
    x_lat = x_sample.reshape(n_sample * sample_len, d)
    n_tokens = n_prompt_tokens + n_sample * sample_len

    cond8 = jnp.zeros((8, d), F32).at[0].set(c_ctx).at[1:1 + n_sample].set(c)
    mod3 = _ada(cond8, w_ada[layer], b_ada[layer]).reshape(8, 6, d)

    w = w_in[layer]
    w_perm = jnp.concatenate([w[:, hw:3 * hw], w[:, :hw], w[:, 3 * hw:]], axis=1).astype(BF16)
    gates, rest = _inproj(x_ctx, x_lat, mod3, norm_g[layer, 0:1], w_perm, sample_len)

    a_np, m_np = _hgrn_consts()
    a_c = jnp.asarray(a_np, BF16)
    m_c = jnp.asarray(m_np, F32)
    gnorm = hgrn_gnorm_g[layer].reshape(1, HEAD_DIM)
    oa0 = jnp.zeros((n_tokens, d), BF16)
    oa1, new_state = _hgrn(gates, rest, 0, n_prompt, prompt_len, hgrn_lb_logits, gnorm, a_c, m_c,
                           None, True, oa0)
    (oa,) = _hgrn(gates, rest, n_prompt_tokens // sample_len, n_sample, sample_len, hgrn_lb_logits,
                  gnorm, a_c, m_c, state_hgrn, False, oa1)

    row = lambda v: v.reshape(1, d)
    x1, h2 = _mix(x_ctx, x_lat, oa, rest, mod3, norm_g[layer, 1:2], conv_w[layer], row(conv_b[layer]),
                  row(conv_ln_g[layer]), row(conv_ln_b[layer]), w_conv_out[layer].astype(BF16),
                  w_hgrn_out[layer].astype(BF16), w_mix_out[layer].astype(BF16), sample_len)

    wq_t = peer_wq[layer].T
    wq_hi = wq_t.astype(BF16)
    wq_lo = (wq_t - wq_hi.astype(F32)).astype(BF16)
    rank2, e2, cnt, bcoef = _topk(h2, wq_hi, wq_lo, peer_keys[layer])

    y_ctx, y_lat = _peer(h2, x1, mod3, row(final_g), rank2, e2, cnt, bcoef, peer_u[layer].astype(BF16),
                         peer_v[layer].T.astype(BF16), n_prompt_tokens, sample_len)
    return (y_ctx.reshape(n_prompt, prompt_len, d), y_lat.reshape(n_sample, sample_len, d),
            new_state)
```

```python
import functools

import numpy as np
import jax
import jax.numpy as jnp
from jax import lax
from jax.experimental import pallas as pl
from jax.experimental.pallas import tpu as pltpu

D_MODEL = 1024
LANES = 128
HEADS = 8
HEAD_DIM = 128
HEADS_PER_STEP = 2
CONV_K = 31
CONV_HALO = 16
CONV_ROWS = 64
PEER_HEADS = 8
PEER_NKEYS = 128
PEER_TOPK = 16
EPS = 1e-6
CHUNK = 128
N_LEVELS = 7
MIX_TILE = 256
N_GATE_BLOCKS = 2
VMEM_LIMIT = 56 * 1024 * 1024

F32 = jnp.float32
BF16 = jnp.bfloat16


def _dot(a, b):
    return jnp.dot(a, b, preferred_element_type=F32)


def _dot_nt(a, b):
    return lax.dot_general(a, b, (((1,), (1,)), ((), ())), preferred_element_type=F32)


def _dot_tn(a, b):
    return lax.dot_general(a, b, (((0,), (0,)), ((), ())), preferred_element_type=F32)


def _split2(x):
    hi = x.astype(BF16)
    lo = (x - hi.astype(F32)).astype(BF16)
    return hi, lo


def _pack_pairs(x):
    return pltpu.bitcast(x.astype(BF16), jnp.uint32)


def _unpack_pairs(x):
    return pltpu.bitcast(x, BF16)


def _sigmoid(x):
    return 1.0 / (1.0 + jnp.exp(-x))


def _rms_rows(x, g):
    ms = jnp.mean(x * x, axis=-1, keepdims=True)
    return x * lax.rsqrt(ms + EPS) * g


def _mod_row(tile, tile_tokens, n_prompt_tokens, sample_len):
    npt = n_prompt_tokens // tile_tokens
    tps = sample_len // tile_tokens
    return jnp.where(tile < npt, 0, 1 + (tile - npt) // tps)


def _ada_kernel(cond_ref, w_ref, b_ref, o_ref):
    c = cond_ref[...]
    s = c * _sigmoid(c)
    o_ref[...] = jnp.dot(s, w_ref[...], preferred_element_type=F32,
                         precision=lax.Precision.HIGHEST) + b_ref[...]


def _ada(cond8, w_ada, b_ada):
    n_out = w_ada.shape[1]
    bn = D_MODEL
    return pl.pallas_call(
        _ada_kernel,
        grid=(n_out // bn,),
        in_specs=[pl.BlockSpec((8, D_MODEL), lambda j: (0, 0)),
                  pl.BlockSpec((D_MODEL, bn), lambda j: (0, j)),
                  pl.BlockSpec((1, bn), lambda j: (0, j))],
        out_specs=pl.BlockSpec((8, bn), lambda j: (0, j)),
        out_shape=jax.ShapeDtypeStruct((8, n_out), F32),
        compiler_params=pltpu.CompilerParams(dimension_semantics=("parallel",),
                                             vmem_limit_bytes=VMEM_LIMIT),
        name="ada",
    )(cond8, w_ada, b_ada.reshape(1, n_out))


def _split_token_specs(tile, n_prompt_tiles):
    ctx = pl.BlockSpec((tile, D_MODEL), lambda i, *_: (jnp.minimum(i, n_prompt_tiles - 1), 0))
    lat = pl.BlockSpec((tile, D_MODEL), lambda i, *_: (jnp.maximum(i - n_prompt_tiles, 0), 0))
    return ctx, lat


def _inproj_kernel(xp_ref, xs_ref, mod_ref, ng_ref, w_ref, gate_ref, rest_ref, h_scr, *, n_prompt_tiles):
    i = pl.program_id(0)
    j = pl.program_id(1)

    def normalise(x_ref):
        y = _rms_rows(x_ref[...], ng_ref[...])
        h = y * (1.0 + mod_ref[0, 1:2, :]) + mod_ref[0, 0:1, :]
        h_scr[...] = h.astype(BF16)

    @pl.when(jnp.logical_and(j == 0, i < n_prompt_tiles))
    def _():
        normalise(xp_ref)

    @pl.when(jnp.logical_and(j == 0, i >= n_prompt_tiles))
    def _():
        normalise(xs_ref)

    @pl.when(j < N_GATE_BLOCKS)
    def _():
        gate_ref[...] = _dot(h_scr[...], w_ref[...])

    @pl.when(j >= N_GATE_BLOCKS)
    def _():
        rest_ref[...] = _dot(h_scr[...], w_ref[...]).astype(BF16)


def _inproj(x_ctx, x_lat, mod3, ng0, w_perm_bf, sample_len):
    n_prompt_tokens = x_ctx.shape[0]
    t = n_prompt_tokens + x_lat.shape[0]
    n_out = w_perm_bf.shape[1]
    tm, tn = 1024, D_MODEL
    nj = n_out // tn
    row = functools.partial(_mod_row, tile_tokens=tm, n_prompt_tokens=n_prompt_tokens,
                            sample_len=sample_len)
    ctx, lat = _split_token_specs(tm, n_prompt_tokens // tm)
    return pl.pallas_call(
        functools.partial(_inproj_kernel, n_prompt_tiles=n_prompt_tokens // tm),
        grid=(t // tm, nj),
        in_specs=[ctx, lat,
                  pl.BlockSpec((1, 6, D_MODEL), lambda i, j: (row(i), 0, 0)),
                  pl.BlockSpec((1, D_MODEL), lambda i, j: (0, 0)),
                  pl.BlockSpec((D_MODEL, tn), lambda i, j: (0, j))],
        out_specs=[pl.BlockSpec((tm, tn), lambda i, j: (i, jnp.minimum(j, N_GATE_BLOCKS - 1))),
                   pl.BlockSpec((tm, tn), lambda i, j: (i, jnp.maximum(j - N_GATE_BLOCKS, 0)))],
        out_shape=[jax.ShapeDtypeStruct((t, N_GATE_BLOCKS * tn), F32),
                   jax.ShapeDtypeStruct((t, n_out - N_GATE_BLOCKS * tn), BF16)],
        scratch_shapes=[pltpu.VMEM((tm, D_MODEL), BF16)],
        compiler_params=pltpu.CompilerParams(dimension_semantics=("parallel", "arbitrary"),
                                             vmem_limit_bytes=VMEM_LIMIT),
        name="inproj",
    )(x_ctx, x_lat, mod3, ng0, w_perm_bf)


def _hgrn_consts():
    c = CHUNK
    t = np.arange(c)[:, None]
    r = np.arange(c)[None, :]
    a_list, m_list = [], []
    for lvl in range(N_LEVELS):
        b = 1 << lvl
        t_base = (t // (2 * b)) * (2 * b)
        bound = t_base + b - 1
        t_right = (t - t_base) >= b
        a_right = (r > bound) & (r <= t)
        a_left = (r > t) & (r <= bound)
        a_list.append(np.where(t_right, a_right, a_left))
        r_base = (r // (2 * b)) * (2 * b)
        m_list.append((t_base == r_base) & t_right & ((r - r_base) < b))
    a_list.append(r <= t)
    a_list.append(r > t)
    m_list.append(t == r)
    a_f = np.stack(a_list).astype(np.float32)
    m_f = np.stack(m_list).astype(np.float32)
    a_b = a_f[:, ::-1, ::-1]
    m_b = m_f[:, ::-1, ::-1]
    a = np.stack([a_f, a_b]).reshape(2, (N_LEVELS + 2) * c, c)
    m = np.stack([m_f, m_b])
    return a, m


def _hgrn_kernel(*refs, n_chunks, has_s0, emit_state):
    q_ref, ff_ref, fb_ref, iv_ref, og_ref, lbl_ref, gn_ref, a_ref, m_ref = refs[:9]
    pos = 9
    s0_ref = None
    if has_s0:
        s0_ref = refs[pos]
        pos += 1
    o_ref = refs[pos]
    pos += 1
    st_ref = None
    if emit_state:
        st_ref = refs[pos]
        pos += 1
    of_scr, ob_scr, s_scr, x_scr, k_scr = refs[pos:pos + 5]
    c = CHUNK
    hd = HEAD_DIM
    nh = HEADS_PER_STEP

    logits = lbl_ref[...]
    e = jnp.exp(logits - jnp.max(logits, axis=0, keepdims=True))
    lb = e[0] / jnp.sum(e, axis=0)

    for d in range(2):
        for hh in range(nh):
            if has_s0:
                s_scr[d, hh] = s0_ref[0, 0, d, hh].T
            else:
                s_scr[d, hh] = jnp.zeros((hd, hd), F32)

    def chunk_rows(d, step):
        ci = step if d == 0 else n_chunks - 1 - step
        return pl.ds(pl.multiple_of(ci * c, c), c)

    def decay_stage(d, step, slot):
        rows = chunk_rows(d, step)
        z = (ff_ref if d == 0 else fb_ref)[rows, :]
        lbd = lb[d:d + 1, :]
        f = lbd + (1.0 - lbd) * _sigmoid(z)
        g1, g2 = _split2(jnp.log(f))
        a = a_ref[d]
        x_scr[d, slot] = jnp.exp(_dot(a, g1) + _dot(a, g2))
        k_scr[d, slot] = (1.0 - lbd) * _sigmoid(-z)

    def matmul_stage(d, step, slot):
        rows = chunk_rows(d, step)
        qv = q_ref[rows, :].astype(F32)
        qs = qv * _sigmoid(qv)
        k = k_scr[d, slot]
        v = iv_ref[rows, :]

        def x_blk(i):
            return x_scr[d, slot, i * c:(i + 1) * c, :]

        x_cum = x_blk(N_LEVELS)
        qb = qs.astype(BF16)
        kb = k.astype(BF16)
        q_cum = (qs * x_cum).astype(BF16)
        k_hi, k_lo = _split2(k * x_blk(N_LEVELS + 1))
        q_lvl = [(qs * x_blk(lvl)).astype(BF16) for lvl in range(N_LEVELS)]
        k_lvl = [(k * x_blk(lvl)).astype(BF16) for lvl in range(N_LEVELS)]
        for hh in range(nh):
            cols = slice(hh * hd, (hh + 1) * hd)
            att = m_ref[d, N_LEVELS] * _dot_nt(qb[:, cols], kb[:, cols])
            for lvl in range(N_LEVELS):
                att = att + m_ref[d, lvl] * _dot_nt(q_lvl[lvl][:, cols], k_lvl[lvl][:, cols])
            st = s_scr[d, hh]
            vh = v[:, cols]
            o = _dot(att.astype(BF16), vh) + _dot_nt(q_cum[:, cols], st.astype(BF16))
            (of_scr if d == 0 else ob_scr)[rows, cols] = o
            total = x_cum[c - 1:c, cols] if d == 0 else x_cum[0:1, cols]
            s_scr[d, hh] = total * st + (_dot_tn(vh, k_hi[:, cols]) + _dot_tn(vh, k_lo[:, cols]))

    def pair(step0, last):
        for par in range(2):
            step = step0 + par
            for d in range(2):
                matmul_stage(d, step, par)
            if not (last and par == 1):
                for d in range(2):
                    decay_stage(d, step + 1, 1 - par)

    for d in range(2):
        decay_stage(d, 0, 0)

    def body(cp, carry):
        pair(2 * cp, False)
        return carry

    if n_chunks > 2:
        lax.fori_loop(0, n_chunks // 2 - 1, body, 0)
    pair(n_chunks - 2, True)

    og = og_ref[...].astype(F32)
    gate = og * _sigmoid(og)
    for hh in range(nh):
        cols = slice(hh * hd, (hh + 1) * hd)
        o = of_scr[:, cols] + ob_scr[:, cols]
        o_ref[:, cols] = (_rms_rows(o, gn_ref[...]) * gate[:, cols]).astype(BF16)
    if emit_state:
        for d in range(2):
            for hh in range(nh):
                st_ref[0, 0, d, hh] = s_scr[d, hh].T


def _hgrn(gates, rest, row_block0, n_seq, seq_len, lb_logits, gnorm, a_c, m_c, s0, emit_state, oa_prev):
    t = gates.shape[0]
    nl = lb_logits.shape[0]
    has_s0 = s0 is not None
    nh = HEADS_PER_STEP
    w = nh * HEAD_DIM
    per = D_MODEL // w

    def col(cb):
        return pl.BlockSpec((seq_len, w), lambda b, h: (row_block0 + b, cb * per + h))

    in_specs = [col(0), col(0), col(1), col(1), col(2),
                pl.BlockSpec((nl, 2, w), lambda b, h: (0, 0, h)),
                pl.BlockSpec((1, HEAD_DIM), lambda b, h: (0, 0)),
                pl.BlockSpec(a_c.shape, lambda b, h: (0, 0, 0)),
                pl.BlockSpec(m_c.shape, lambda b, h: (0, 0, 0, 0))]
    args = [rest, gates, gates, rest, rest, lb_logits, gnorm, a_c, m_c]
    state_block = (1, 1, 2, nh, HEAD_DIM, HEAD_DIM)
    if has_s0:
        in_specs.append(pl.BlockSpec(state_block, lambda b, h: (b, 0, 0, h, 0, 0)))
        args.append(s0)
    in_specs.append(pl.BlockSpec(memory_space=pl.ANY))
    args.append(oa_prev)
    out_specs = [pl.BlockSpec((seq_len, w), lambda b, h: (row_block0 + b, h))]
    out_shape = [jax.ShapeDtypeStruct((t, D_MODEL), BF16)]
    if emit_state:
        out_specs.append(pl.BlockSpec(state_block, lambda b, h: (b, 0, 0, h, 0, 0)))
        out_shape.append(jax.ShapeDtypeStruct((n_seq, 1, 2, HEADS, HEAD_DIM, HEAD_DIM), F32))

    def kern(*refs):
        n_in = len(args)
        refs = refs[:n_in - 1] + refs[n_in:]
        _hgrn_kernel(*refs, n_chunks=seq_len // CHUNK, has_s0=has_s0, emit_state=emit_state)

    outs = pl.pallas_call(
        kern,
        grid=(n_seq, HEADS // nh),
        in_specs=in_specs,
        out_specs=out_specs,
        out_shape=out_shape,
        scratch_shapes=[pltpu.VMEM((seq_len, w), F32),
                        pltpu.VMEM((seq_len, w), F32),
                        pltpu.VMEM((2, nh, HEAD_DIM, HEAD_DIM), F32),
                        pltpu.VMEM((2, 2, (N_LEVELS + 2) * CHUNK, w), F32),
                        pltpu.VMEM((2, 2, CHUNK, w), F32)],
        input_output_aliases={len(args) - 1: 0},
        compiler_params=pltpu.CompilerParams(dimension_semantics=("parallel", "parallel"),
                                             vmem_limit_bytes=VMEM_LIMIT),
        name="hgrn_ctx" if emit_state else "hgrn_lat",
    )(*args)
    return outs


def _mix_kernel(xp_ref, xs_ref, oa_ref, ga_ref, gb_ref, gta_ref, gtb_ref, gap_ref, gan_ref, gbp_ref, gbn_ref,
                mod_ref, ng_ref, cw_ref, cb_ref, lng_ref, lnb_ref, wco_ref, who_ref, wmo_ref,
                x1_ref, h2_ref, u_scr, conv_scr, *, n_prompt_tiles, tiles_per_sample):
    tm = MIX_TILE
    hl = CONV_HALO
    i = pl.program_id(0)
    j = lax.rem(jnp.maximum(i - n_prompt_tiles, 0), tiles_per_sample)
    is_prompt = i < n_prompt_tiles
    keep_prev = jnp.where(jnp.logical_or(is_prompt, j == 0), 0.0, 1.0)
    keep_next = jnp.where(jnp.logical_or(is_prompt, j == tiles_per_sample - 1), 0.0, 1.0)

    def glu(a_ref, b_ref):
        return a_ref[...].astype(F32) * _sigmoid(b_ref[...].astype(F32))

    u_scr[0:hl, :] = glu(gap_ref, gbp_ref) * keep_prev
    u_scr[hl:hl + tm, :] = glu(ga_ref, gb_ref)
    u_scr[hl + tm:hl + tm + hl, :] = glu(gan_ref, gbn_ref) * keep_next

    off0 = hl - CONV_K // 2
    rb = CONV_ROWS
    span = rb + 8 * (-(-(CONV_K + off0) // 8))
    for cblk in range(D_MODEL // LANES):
        cols = slice(cblk * LANES, (cblk + 1) * LANES)
        wv = cw_ref[:, cols]
        taps = [jnp.broadcast_to(wv[k:k + 1, :], (rb, LANES)) for k in range(CONV_K)]
        bias = jnp.broadcast_to(cb_ref[:, cols], (rb, LANES))
        for r0 in range(0, tm, rb):
            ext = u_scr[r0:r0 + span, cols]
            acc = bias
            for s in range(8):
                us = ext if s == 0 else pltpu.roll(ext, span - s, axis=0)
                for q in range(span // 8):
                    k = 8 * q + s - off0
                    if 0 <= k < CONV_K:
                        acc = acc + taps[k] * us[8 * q:8 * q + rb, :]
            conv_scr[r0:r0 + rb, cols] = acc

    acc = conv_scr[...]
    mu = jnp.mean(acc, axis=-1, keepdims=True)
    cen = acc - mu
    var = jnp.mean(cen * cen, axis=-1, keepdims=True)
    y = cen * lax.rsqrt(var + EPS) * lng_ref[...] + lnb_ref[...]
    u2 = y * _sigmoid(y)
    y_b = _dot(u2.astype(BF16), wco_ref[...])
    y_a = _dot(oa_ref[...], who_ref[...])
    mix = _sigmoid(gta_ref[...].astype(F32)) * y_a + _sigmoid(gtb_ref[...].astype(F32)) * y_b
    branch = mod_ref[0, 2:3, :] * _dot(mix.astype(BF16), wmo_ref[...])

    def residual(x_ref):
        x1 = x_ref[...] + branch
        x1_ref[...] = x1
        h2_ref[...] = _rms_rows(x1, ng_ref[...]) * (1.0 + mod_ref[0, 4:5, :]) + mod_ref[0, 3:4, :]

    @pl.when(is_prompt)
    def _():
        residual(xp_ref)

    @pl.when(jnp.logical_not(is_prompt))
    def _():
        residual(xs_ref)


def _mix(x_ctx, x_lat, oa, rest, mod3, ng1, conv_w, conv_b, ln_g, ln_b, wco, who, wmo,
         sample_len):
    n_prompt_tokens = x_ctx.shape[0]
    t = n_prompt_tokens + x_lat.shape[0]
    tm = MIX_TILE
    hl = CONV_HALO
    r = tm // hl
    n_halo_blocks = t // hl
    col0 = 3
    row = functools.partial(_mod_row, tile_tokens=tm, n_prompt_tokens=n_prompt_tokens,
                            sample_len=sample_len)

    def tile(cb):
        return pl.BlockSpec((tm, D_MODEL), lambda i: (i, cb))

    def prev(cb):
        return pl.BlockSpec((hl, D_MODEL), lambda i: (jnp.maximum(i * r - 1, 0), cb))

    def nxt(cb):
        return pl.BlockSpec((hl, D_MODEL), lambda i: (jnp.minimum((i + 1) * r, n_halo_blocks - 1), cb))

    def full(shape):
        return pl.BlockSpec(shape, lambda i: (0,) * len(shape))

    ctx, lat = _split_token_specs(tm, n_prompt_tokens // tm)
    kern = functools.partial(_mix_kernel, n_prompt_tiles=n_prompt_tokens // tm,
                             tiles_per_sample=sample_len // tm)
    return pl.pallas_call(
        kern,
        grid=(t // tm,),
        in_specs=[ctx, lat, tile(0), tile(col0), tile(col0 + 1), tile(col0 + 2), tile(col0 + 3),
                  prev(col0), nxt(col0), prev(col0 + 1), nxt(col0 + 1),
                  pl.BlockSpec((1, 6, D_MODEL), lambda i: (row(i), 0, 0)),
                  full((1, D_MODEL)), full((CONV_K, D_MODEL)), full((1, D_MODEL)),
                  full((1, D_MODEL)), full((1, D_MODEL)),
                  full((D_MODEL, D_MODEL)), full((D_MODEL, D_MODEL)), full((D_MODEL, D_MODEL))],
        out_specs=[tile(0), tile(0)],
        out_shape=[jax.ShapeDtypeStruct((t, D_MODEL), F32), jax.ShapeDtypeStruct((t, D_MODEL), F32)],
        scratch_shapes=[pltpu.VMEM((tm + 2 * hl, D_MODEL), F32),
                        pltpu.VMEM((tm, D_MODEL), F32)],
        compiler_params=pltpu.CompilerParams(dimension_semantics=("parallel",),
                                             vmem_limit_bytes=VMEM_LIMIT),
        name="mix",
    )(x_ctx, x_lat, oa, rest, rest, rest, rest, rest, rest, rest, rest, mod3, ng1, conv_w, conv_b, ln_g, ln_b,
      wco, who, wmo)


def _cand_rows():
    return [(a, PEER_TOPK // (a + 1)) for a in range(PEER_TOPK)]


N_CAND = sum(nb for _, nb in _cand_rows())
N_CAND_PAD = -(-N_CAND // 8) * 8


def _topk_kernel(h2_ref, wqh_ref, wql_ref, keys_ref, rank_ref, e2_ref, n_ref, b_ref,
                 q_scr, hh_scr, hl_scr, v_scr, cand_scr):
    nk = PEER_NKEYS
    tm = h2_ref.shape[0]
    hh, hl = _split2(h2_ref[...])
    hh_scr[...] = hh
    hl_scr[...] = hl
    neg = -jnp.inf

    def project(h, slot):
        rows = pl.ds(pl.multiple_of(h * 2 * nk, 2 * nk), 2 * nk)
        wqh = wqh_ref[rows, :]
        q_scr[slot] = (_dot_nt(wqh, hh_scr[...]) + _dot_nt(wqh, hl_scr[...])
                       + _dot_nt(wql_ref[rows, :], hh_scr[...]))

    def select(h, slot):
        kparts = [_split2(keys_ref[h, p]) for p in range(2)]
        for lt in range(tm // LANES):
            lanes = slice(lt * LANES, (lt + 1) * LANES)
            s = []
            for p in range(2):
                kh, kl = kparts[p]
                qh, ql = _split2(q_scr[slot, p * nk:(p + 1) * nk, lanes])
                sp = _dot(kh, qh) + _dot(kh, ql) + _dot(kl, qh)
                s.append(sp)
                cur = sp
                rank = jnp.full(sp.shape, float(PEER_TOPK), F32)
                for a in range(PEER_TOPK):
                    m = jnp.max(cur, axis=0, keepdims=True)
                    hit = cur == m
                    if p == 1:
                        rank = jnp.where(hit, float(a), rank)
                    cur = jnp.where(hit, neg, cur)
                    v_scr[p, a:a + 1, :] = m
                if p == 1:
                    rank_ref[h, :, lanes] = _pack_pairs(rank)
            s1, s2 = s
            v1 = v_scr[0]
            v2 = v_scr[1]
            off = 0
            for a, nb in _cand_rows():
                cand_scr[off:off + nb, :] = v1[a:a + 1, :] + v2[0:nb, :]
                off += nb
            if N_CAND_PAD > N_CAND:
                cand_scr[N_CAND:N_CAND_PAD, :] = jnp.full((N_CAND_PAD - N_CAND, LANES), neg, F32)
            cand = cand_scr[...]
            cur = cand
            tau = None
            for a in range(PEER_TOPK):
                tau = jnp.max(cur, axis=0, keepdims=True)
                if a < PEER_TOPK - 1:
                    cur = jnp.where(cur == tau, neg, cur)
            top = v1[0:1, :] + v2[0:1, :]
            zsum = jnp.sum(jnp.where(cand >= tau, jnp.exp(cand - top), 0.0), axis=0, keepdims=True)
            cnt = jnp.zeros(s1.shape, F32)
            for b in range(PEER_TOPK):
                cnt = cnt + jnp.where(s1 + v2[b:b + 1, :] >= tau, 1.0, 0.0)
            n_ref[h, :, lanes] = cnt
            b_ref[h, :, lanes] = jnp.exp(s1 - v1[0:1, :]) * (0.5 / zsum)
            e2_ref[h, :, lanes] = _pack_pairs(jnp.exp(s2 - v2[0:1, :]))

    def pair(h0, last):
        for par in range(2):
            h = h0 + par
            if not (last and par == 1):
                project(h + 1, 1 - par)
            select(h, par)

    project(0, 0)

    def body(hp, carry):
        pair(2 * hp, False)
        return carry

    lax.fori_loop(0, PEER_HEADS // 2 - 1, body, 0)
    pair(PEER_HEADS - 2, True)


def _topk(h2, wq_t_hi, wq_t_lo, keys):
    t = h2.shape[0]
    tm = 256
    nq = wq_t_hi.shape[0]
    route = pl.BlockSpec((PEER_HEADS, PEER_NKEYS, tm), lambda i: (0, 0, i))
    packed = pl.BlockSpec((PEER_HEADS, PEER_NKEYS // 2, tm), lambda i: (0, 0, i))

    def route_shape(dt):
        return jax.ShapeDtypeStruct((PEER_HEADS, PEER_NKEYS, t), dt)

    packed_shape = jax.ShapeDtypeStruct((PEER_HEADS, PEER_NKEYS // 2, t), jnp.uint32)

    return pl.pallas_call(
        _topk_kernel,
        grid=(t // tm,),
        in_specs=[pl.BlockSpec((tm, D_MODEL), lambda i: (i, 0)),
                  pl.BlockSpec((nq, D_MODEL), lambda i: (0, 0)),
                  pl.BlockSpec((nq, D_MODEL), lambda i: (0, 0)),
                  pl.BlockSpec(keys.shape, lambda i: (0, 0, 0, 0))],
        out_specs=[packed, packed, route, route],
        out_shape=[packed_shape, packed_shape, route_shape(F32), route_shape(F32)],
        scratch_shapes=[pltpu.VMEM((2, 2 * PEER_NKEYS, tm), F32),
                        pltpu.VMEM((tm, D_MODEL), BF16),
                        pltpu.VMEM((tm, D_MODEL), BF16),
                        pltpu.VMEM((2, PEER_TOPK, LANES), F32),
                        pltpu.VMEM((N_CAND_PAD, LANES), F32)],
        compiler_params=pltpu.CompilerParams(dimension_semantics=("parallel",),
                                             vmem_limit_bytes=VMEM_LIMIT),
        name="peer_topk",
    )(h2, wq_t_hi, wq_t_lo, keys)


PEER_ROWS_PER_HALF = 8
PEER_ROW_GROUP = 4


def _peer_kernel(h2_ref, x1_ref, mod_ref, fg_ref, rank_ref, e2_ref, n_ref, b_ref, u_ref, vta_ref, vtb_ref,
                 yp_ref, ys_ref, acc_scr, xb_scr, a0_scr, a1_scr, g0_scr, g1_scr, w0_scr, w1_scr, *,
                 n_prompt_tiles):
    nk = PEER_NKEYS
    rph = PEER_ROWS_PER_HALF
    nbh = rph * nk
    tm = h2_ref.shape[0]
    s = pl.program_id(1)

    @pl.when(s == 0)
    def _():
        acc_scr[...] = jnp.zeros(acc_scr.shape, F32)
        xb_scr[...] = h2_ref[...].astype(BF16)
        g1_scr[...] = jnp.zeros(g1_scr.shape, BF16)

    zero = jnp.zeros((nk, LANES), BF16)

    def packed_row(ref, h, ii, lanes):
        return jnp.broadcast_to(ref[h, ii:ii + 1, lanes], (nk, LANES)).astype(BF16)

    def gate_weights(row0, w_scr):
        for lt in range(tm // LANES):
            lanes = slice(lt * LANES, (lt + 1) * LANES)
            for g0 in range(0, rph, PEER_ROW_GROUP):
                w = [zero] * PEER_ROW_GROUP
                for h in range(PEER_HEADS):
                    rk = _unpack_pairs(rank_ref[h, :, lanes])
                    ev = _unpack_pairs(e2_ref[h, :, lanes])
                    for r in range(PEER_ROW_GROUP):
                        cnt = packed_row(n_ref, h, row0 + g0 + r, lanes)
                        coef = packed_row(b_ref, h, row0 + g0 + r, lanes)
                        w[r] = w[r] + jnp.where(rk < cnt, ev * coef, zero)
                for r in range(PEER_ROW_GROUP):
                    w_scr[(g0 + r) * nk:(g0 + r + 1) * nk, lanes] = w[r]

    def activate(a_scr, w_scr, g_scr):
        a = a_scr[...]
        gel = a * (1.0 + lax.erf(a * 0.7071067811865476))
        g_scr[...] = w_scr[...] * gel.astype(BF16)

    xb = xb_scr[...]
    gate_weights(0, w0_scr)
    acc_scr[...] += _dot(vta_ref[...], g1_scr[...])
    a0_scr[...] = _dot_nt(u_ref[0:nbh, :], xb)
    a1_scr[...] = _dot_nt(u_ref[nbh:2 * nbh, :], xb)
    activate(a0_scr, w0_scr, g0_scr)
    gate_weights(rph, w1_scr)
    acc_scr[...] += _dot(vtb_ref[:, 0:nbh], g0_scr[...])
    activate(a1_scr, w1_scr, g1_scr)

    def finish(y_ref):
        pe = acc_scr[...] + _dot(vtb_ref[:, nbh:2 * nbh], g1_scr[...])
        x2 = x1_ref[...] + mod_ref[0, 5:6, :] * pe.T
        y_ref[...] = _rms_rows(x2, fg_ref[...])

    last = s == pl.num_programs(1) - 1
    is_prompt = pl.program_id(0) < n_prompt_tiles

    @pl.when(jnp.logical_and(last, is_prompt))
    def _():
        finish(yp_ref)

    @pl.when(jnp.logical_and(last, jnp.logical_not(is_prompt)))
    def _():
        finish(ys_ref)


def _peer(h2, x1, mod3, final_g, rank2, e2, cnt, bcoef, u_bf, vt_bf, n_prompt_tokens, sample_len):
    t = h2.shape[0]
    tm = 512
    rph = PEER_ROWS_PER_HALF
    nbh = rph * PEER_NKEYS
    n_exp = u_bf.shape[0]
    row = functools.partial(_mod_row, tile_tokens=tm, n_prompt_tokens=n_prompt_tokens,
                            sample_len=sample_len)
    route = pl.BlockSpec((PEER_HEADS, PEER_NKEYS // 2, tm), lambda i, s: (0, 0, i))
    route_rows = pl.BlockSpec((PEER_HEADS, 2 * rph, tm), lambda i, s: (0, s, i))
    tok = pl.BlockSpec((tm, D_MODEL), lambda i, s: (i, 0))
    return pl.pallas_call(
        functools.partial(_peer_kernel, n_prompt_tiles=n_prompt_tokens // tm),
        grid=(t // tm, n_exp // (2 * nbh)),
        in_specs=[tok, tok,
                  pl.BlockSpec((1, 6, D_MODEL), lambda i, s: (row(i), 0, 0)),
                  pl.BlockSpec((1, D_MODEL), lambda i, s: (0, 0)),
                  route, route, route_rows, route_rows,
                  pl.BlockSpec((2 * nbh, D_MODEL), lambda i, s: (s, 0)),
                  pl.BlockSpec((D_MODEL, nbh), lambda i, s: (0, jnp.maximum(2 * s - 1, 0))),
                  pl.BlockSpec((D_MODEL, 2 * nbh), lambda i, s: (0, s))],
        out_specs=list(_split_token_specs(tm, n_prompt_tokens // tm)),
        out_shape=[jax.ShapeDtypeStruct((n_prompt_tokens, D_MODEL), F32),
                   jax.ShapeDtypeStruct((t - n_prompt_tokens, D_MODEL), F32)],
        scratch_shapes=[pltpu.VMEM((D_MODEL, tm), F32),
                        pltpu.VMEM((tm, D_MODEL), BF16),
                        pltpu.VMEM((nbh, tm), F32),
                        pltpu.VMEM((nbh, tm), F32),
                        pltpu.VMEM((nbh, tm), BF16),
                        pltpu.VMEM((nbh, tm), BF16),
                        pltpu.VMEM((nbh, tm), BF16),
                        pltpu.VMEM((nbh, tm), BF16)],
        compiler_params=pltpu.CompilerParams(dimension_semantics=("arbitrary", "arbitrary"),
                                             vmem_limit_bytes=VMEM_LIMIT),
        name="peer_experts",
    )(h2, x1, mod3, final_g, rank2, e2, cnt, bcoef, u_bf, vt_bf, vt_bf)


def kernel(x_prompt, x_sample, state_hgrn, c, c_ctx, norm_g, w_ada, b_ada, w_in, hgrn_lb_logits, hgrn_gnorm_g, w_hgrn_out, conv_w, conv_b, conv_ln_g, conv_ln_b, w_conv_out, w_mix_out, peer_wq, peer_keys, peer_u, peer_v, final_g):
    n_prompt, prompt_len, d = x_prompt.shape
    n_sample, sample_len, _ = x_sample.shape
    depth = w_in.shape[0]
    assert d == D_MODEL and depth == 1 and prompt_len == MIX_TILE
    assert n_sample + 1 <= 8 and sample_len % MIX_TILE == 0
    n_prompt_tokens = n_prompt * prompt_len
    assert n_prompt_tokens % sample_len == 0
    layer = 0
    hw = HEADS * HEAD_DIM

    x_ctx = x_prompt.reshape(n_prompt_tokens, d)
    x_lat = x_sample.reshape(n_sample * sample_len, d)
    n_tokens = n_prompt_tokens + n_sample * sample_len

    cond8 = jnp.zeros((8, d), F32).at[0].set(c_ctx).at[1:1 + n_sample].set(c)
    mod3 = _ada(cond8, w_ada[layer], b_ada[layer]).reshape(8, 6, d)

    w = w_in[layer]
    w_perm = jnp.concatenate([w[:, hw:3 * hw], w[:, :hw], w[:, 3 * hw:]], axis=1).astype(BF16)
    gates, rest = _inproj(x_ctx, x_lat, mod3, norm_g[layer, 0:1], w_perm, sample_len)

    a_np, m_np = _hgrn_consts()
    a_c = jnp.asarray(a_np, BF16)
    m_c = jnp.asarray(m_np, F32)
    gnorm = hgrn_gnorm_g[layer].reshape(1, HEAD_DIM)
    oa0 = jnp.zeros((n_tokens, d), BF16)
    oa1, new_state = _hgrn(gates, rest, 0, n_prompt, prompt_len, hgrn_lb_logits, gnorm, a_c, m_c,
                           None, True, oa0)
    (oa,) = _hgrn(gates, rest, n_prompt_tokens // sample_len, n_sample, sample_len, hgrn_lb_logits,
                  gnorm, a_c, m_c, state_hgrn, False, oa1)

    row = lambda v: v.reshape(1, d)
    x1, h2 = _mix(x_ctx, x_lat, oa, rest, mod3, norm_g[layer, 1:2], conv_w[layer], row(conv_b[layer]),
                  row(conv_ln_g[layer]), row(conv_ln_b[layer]), w_conv_out[layer].astype(BF16),
                  w_hgrn_out[layer].astype(BF16), w_mix_out[layer].astype(BF16), sample_len)

    wq_t = peer_wq[layer].T
    wq_hi = wq_t.astype(BF16)
    wq_lo = (wq_t - wq_hi.astype(F32)).astype(BF16)
    rank2, e2, cnt, bcoef = _topk(h2, wq_hi, wq_lo, peer_keys[layer])

    y_ctx, y_lat = _peer(h2, x1, mod3, row(final_g), rank2, e2, cnt, bcoef, peer_u[layer].astype(BF16),
                         peer_v[layer].T.astype(BF16), n_prompt_tokens, sample_len)
    return (y_ctx.reshape(n_prompt, prompt_len, d), y_lat.reshape(n_sample, sample_len, d),
            new_state)
```

```python
import functools

import numpy as np
import jax
import jax.numpy as jnp
from jax import lax
from jax.experimental import pallas as pl
from jax.experimental.pallas import tpu as pltpu

D_MODEL = 1024
LANES = 128
HEADS = 8
HEAD_DIM = 128
HEADS_PER_STEP = 2
CONV_K = 31
CONV_HALO = 16
CONV_ROWS = 64
PEER_HEADS = 8
PEER_NKEYS = 128
PEER_TOPK = 16
EPS = 1e-6
CHUNK = 128
N_LEVELS = 7
MIX_TILE = 256
N_GATE_BLOCKS = 2
VMEM_LIMIT = 56 * 1024 * 1024

F32 = jnp.float32
BF16 = jnp.bfloat16


def _dot(a, b):
    return jnp.dot(a, b, preferred_element_type=F32)


def _dot_nt(a, b):
    return lax.dot_general(a, b, (((1,), (1,)), ((), ())), preferred_element_type=F32)


def _dot_tn(a, b):
    return lax.dot_general(a, b, (((0,), (0,)), ((), ())), preferred_element_type=F32)


def _split2(x):
    hi = x.astype(BF16)
    lo = (x - hi.astype(F32)).astype(BF16)
    return hi, lo


def _pack_pairs(x):
    return pltpu.bitcast(x.astype(BF16), jnp.uint32)


def _unpack_pairs(x):
    return pltpu.bitcast(x, BF16)


def _sigmoid(x):
    return 1.0 / (1.0 + jnp.exp(-x))


def _rms_rows(x, g):
    ms = jnp.mean(x * x, axis=-1, keepdims=True)
    return x * lax.rsqrt(ms + EPS) * g


def _mod_row(tile, tile_tokens, n_prompt_tokens, sample_len):
    npt = n_prompt_tokens // tile_tokens
    tps = sample_len // tile_tokens
    return jnp.where(tile < npt, 0, 1 + (tile - npt) // tps)


def _ada_kernel(cond_ref, w_ref, b_ref, o_ref):
    c = cond_ref[...]
    s = c * _sigmoid(c)
    o_ref[...] = jnp.dot(s, w_ref[...], preferred_element_type=F32,
                         precision=lax.Precision.HIGHEST) + b_ref[...]


def _ada(cond8, w_ada, b_ada):
    n_out = w_ada.shape[1]
    bn = D_MODEL
    return pl.pallas_call(
        _ada_kernel,
        grid=(n_out // bn,),
        in_specs=[pl.BlockSpec((8, D_MODEL), lambda j: (0, 0)),
                  pl.BlockSpec((D_MODEL, bn), lambda j: (0, j)),
                  pl.BlockSpec((1, bn), lambda j: (0, j))],
        out_specs=pl.BlockSpec((8, bn), lambda j: (0, j)),
        out_shape=jax.ShapeDtypeStruct((8, n_out), F32),
        compiler_params=pltpu.CompilerParams(dimension_semantics=("parallel",),
                                             vmem_limit_bytes=VMEM_LIMIT),
        name="ada",
    )(cond8, w_ada, b_ada.reshape(1, n_out))


def _split_token_specs(tile, n_prompt_tiles):
    ctx = pl.BlockSpec((tile, D_MODEL), lambda i, *_: (jnp.minimum(i, n_prompt_tiles - 1), 0))
    lat = pl.BlockSpec((tile, D_MODEL), lambda i, *_: (jnp.maximum(i - n_prompt_tiles, 0), 0))
    return ctx, lat


def _inproj_kernel(xp_ref, xs_ref, mod_ref, ng_ref, w_ref, gate_ref, rest_ref, h_scr, *, n_prompt_tiles):
    i = pl.program_id(0)
    j = pl.program_id(1)

    def normalise(x_ref):
        y = _rms_rows(x_ref[...], ng_ref[...])
        h = y * (1.0 + mod_ref[0, 1:2, :]) + mod_ref[0, 0:1, :]
        h_scr[...] = h.astype(BF16)

    @pl.when(jnp.logical_and(j == 0, i < n_prompt_tiles))
    def _():
        normalise(xp_ref)

    @pl.when(jnp.logical_and(j == 0, i >= n_prompt_tiles))
    def _():
        normalise(xs_ref)

    @pl.when(j < N_GATE_BLOCKS)
    def _():
        gate_ref[...] = _dot(h_scr[...], w_ref[...])

    @pl.when(j >= N_GATE_BLOCKS)
    def _():
        rest_ref[...] = _dot(h_scr[...], w_ref[...]).astype(BF16)


def _inproj(x_ctx, x_lat, mod3, ng0, w_perm_bf, sample_len):
    n_prompt_tokens = x_ctx.shape[0]
    t = n_prompt_tokens + x_lat.shape[0]
    n_out = w_perm_bf.shape[1]
    tm, tn = 1024, D_MODEL
    nj = n_out // tn
    row = functools.partial(_mod_row, tile_tokens=tm, n_prompt_tokens=n_prompt_tokens,
                            sample_len=sample_len)
    ctx, lat = _split_token_specs(tm, n_prompt_tokens // tm)
    return pl.pallas_call(
        functools.partial(_inproj_kernel, n_prompt_tiles=n_prompt_tokens // tm),
        grid=(t // tm, nj),
        in_specs=[ctx, lat,
                  pl.BlockSpec((1, 6, D_MODEL), lambda i, j: (row(i), 0, 0)),
                  pl.BlockSpec((1, D_MODEL), lambda i, j: (0, 0)),
                  pl.BlockSpec((D_MODEL, tn), lambda i, j: (0, j))],
        out_specs=[pl.BlockSpec((tm, tn), lambda i, j: (i, jnp.minimum(j, N_GATE_BLOCKS - 1))),
                   pl.BlockSpec((tm, tn), lambda i, j: (i, jnp.maximum(j - N_GATE_BLOCKS, 0)))],
        out_shape=[jax.ShapeDtypeStruct((t, N_GATE_BLOCKS * tn), F32),
                   jax.ShapeDtypeStruct((t, n_out - N_GATE_BLOCKS * tn), BF16)],
        scratch_shapes=[pltpu.VMEM((tm, D_MODEL), BF16)],
        compiler_params=pltpu.CompilerParams(dimension_semantics=("parallel", "arbitrary"),
                                             vmem_limit_bytes=VMEM_LIMIT),
        name="inproj",
    )(x_ctx, x_lat, mod3, ng0, w_perm_bf)


def _hgrn_consts():
    c = CHUNK
    t = np.arange(c)[:, None]
    r = np.arange(c)[None, :]
    a_list, m_list = [], []
    for lvl in range(N_LEVELS):
        b = 1 << lvl
        t_base = (t // (2 * b)) * (2 * b)
        bound = t_base + b - 1
        t_right = (t - t_base) >= b
        a_right = (r > bound) & (r <= t)
        a_left = (r > t) & (r <= bound)
        a_list.append(np.where(t_right, a_right, a_left))
        r_base = (r // (2 * b)) * (2 * b)
        m_list.append((t_base == r_base) & t_right & ((r - r_base) < b))
    a_list.append(r <= t)
    a_list.append(r > t)
    m_list.append(t == r)
    a_f = np.stack(a_list).astype(np.float32)
    m_f = np.stack(m_list).astype(np.float32)
    a_b = a_f[:, ::-1, ::-1]
    m_b = m_f[:, ::-1, ::-1]
    a = np.stack([a_f, a_b]).reshape(2, (N_LEVELS + 2) * c, c)
    m = np.stack([m_f, m_b])
    return a, m


def _hgrn_kernel(*refs, n_chunks, has_s0, emit_state):
    q_ref, ff_ref, fb_ref, iv_ref, og_ref, lbl_ref, gn_ref, a_ref, m_ref = refs[:9]
    pos = 9
    s0_ref = None
    if has_s0:
        s0_ref = refs[pos]
        pos += 1
    o_ref = refs[pos]
    pos += 1
    st_ref = None
    if emit_state:
        st_ref = refs[pos]
        pos += 1
    of_scr, ob_scr, s_scr, x_scr, k_scr = refs[pos:pos + 5]
    c = CHUNK
    hd = HEAD_DIM
    nh = HEADS_PER_STEP

    logits = lbl_ref[...]
    e = jnp.exp(logits - jnp.max(logits, axis=0, keepdims=True))
    lb = e[0] / jnp.sum(e, axis=0)

    for d in range(2):
        for hh in range(nh):
            if has_s0:
                s_scr[d, hh] = s0_ref[0, 0, d, hh].T
            else:
                s_scr[d, hh] = jnp.zeros((hd, hd), F32)

    def chunk_rows(d, step):
        ci = step if d == 0 else n_chunks - 1 - step
        return pl.ds(pl.multiple_of(ci * c, c), c)

    def decay_stage(d, step, slot):
        rows = chunk_rows(d, step)
        z = (ff_ref if d == 0 else fb_ref)[rows, :]
        lbd = lb[d:d + 1, :]
        f = lbd + (1.0 - lbd) * _sigmoid(z)
        g1, g2 = _split2(jnp.log(f))
        a = a_ref[d]
        x_scr[d, slot] = jnp.exp(_dot(a, g1) + _dot(a, g2))
        k_scr[d, slot] = (1.0 - lbd) * _sigmoid(-z)

    def matmul_stage(d, step, slot):
        rows = chunk_rows(d, step)
        qv = q_ref[rows, :].astype(F32)
        qs = qv * _sigmoid(qv)
        k = k_scr[d, slot]
        v = iv_ref[rows, :]

        def x_blk(i):
            return x_scr[d, slot, i * c:(i + 1) * c, :]

        x_cum = x_blk(N_LEVELS)
        qb = qs.astype(BF16)
        kb = k.astype(BF16)
        q_cum = (qs * x_cum).astype(BF16)
        k_hi, k_lo = _split2(k * x_blk(N_LEVELS + 1))
        q_lvl = [(qs * x_blk(lvl)).astype(BF16) for lvl in range(N_LEVELS)]
        k_lvl = [(k * x_blk(lvl)).astype(BF16) for lvl in range(N_LEVELS)]
        for hh in range(nh):
            cols = slice(hh * hd, (hh + 1) * hd)
            att = m_ref[d, N_LEVELS] * _dot_nt(qb[:, cols], kb[:, cols])
            for lvl in range(N_LEVELS):
                att = att + m_ref[d, lvl] * _dot_nt(q_lvl[lvl][:, cols], k_lvl[lvl][:, cols])
            st = s_scr[d, hh]
            vh = v[:, cols]
            o = _dot(att.astype(BF16), vh) + _dot_nt(q_cum[:, cols], st.astype(BF16))
            (of_scr if d == 0 else ob_scr)[rows, cols] = o
            total = x_cum[c - 1:c, cols] if d == 0 else x_cum[0:1, cols]
            s_scr[d, hh] = total * st + (_dot_tn(vh, k_hi[:, cols]) + _dot_tn(vh, k_lo[:, cols]))

    def pair(step0, last):
        for par in range(2):
            step = step0 + par
            for d in range(2):
                matmul_stage(d, step, par)
            if not (last and par == 1):
                for d in range(2):
                    decay_stage(d, step + 1, 1 - par)

    for d in range(2):
        decay_stage(d, 0, 0)

    def body(cp, carry):
        pair(2 * cp, False)
        return carry

    if n_chunks > 2:
        lax.fori_loop(0, n_chunks // 2 - 1, body, 0)
    pair(n_chunks - 2, True)

    og = og_ref[...].astype(F32)
    gate = og * _sigmoid(og)
    for hh in range(nh):
        cols = slice(hh * hd, (hh + 1) * hd)
        o = of_scr[:, cols] + ob_scr[:, cols]
        o_ref[:, cols] = (_rms_rows(o, gn_ref[...]) * gate[:, cols]).astype(BF16)
    if emit_state:
        for d in range(2):
            for hh in range(nh):
                st_ref[0, 0, d, hh] = s_scr[d, hh].T


def _hgrn(gates, rest, row_block0, n_seq, seq_len, lb_logits, gnorm, a_c, m_c, s0, emit_state, oa_prev):
    t = gates.shape[0]
    nl = lb_logits.shape[0]
    has_s0 = s0 is not None
    nh = HEADS_PER_STEP
    w = nh * HEAD_DIM
    per = D_MODEL // w

    def col(cb):
        return pl.BlockSpec((seq_len, w), lambda b, h: (row_block0 + b, cb * per + h))

    in_specs = [col(0), col(0), col(1), col(1), col(2),
                pl.BlockSpec((nl, 2, w), lambda b, h: (0, 0, h)),
                pl.BlockSpec((1, HEAD_DIM), lambda b, h: (0, 0)),
                pl.BlockSpec(a_c.shape, lambda b, h: (0, 0, 0)),
                pl.BlockSpec(m_c.shape, lambda b, h: (0, 0, 0, 0))]
    args = [rest, gates, gates, rest, rest, lb_logits, gnorm, a_c, m_c]
    state_block = (1, 1, 2, nh, HEAD_DIM, HEAD_DIM)
    if has_s0:
        in_specs.append(pl.BlockSpec(state_block, lambda b, h: (b, 0, 0, h, 0, 0)))
        args.append(s0)
    in_specs.append(pl.BlockSpec(memory_space=pl.ANY))
    args.append(oa_prev)
    out_specs = [pl.BlockSpec((seq_len, w), lambda b, h: (row_block0 + b, h))]
    out_shape = [jax.ShapeDtypeStruct((t, D_MODEL), BF16)]
    if emit_state:
        out_specs.append(pl.BlockSpec(state_block, lambda b, h: (b, 0, 0, h, 0, 0)))
        out_shape.append(jax.ShapeDtypeStruct((n_seq, 1, 2, HEADS, HEAD_DIM, HEAD_DIM), F32))

    def kern(*refs):
        n_in = len(args)
        refs = refs[:n_in - 1] + refs[n_in:]
        _hgrn_kernel(*refs, n_chunks=seq_len // CHUNK, has_s0=has_s0, emit_state=emit_state)

    outs = pl.pallas_call(
        kern,
        grid=(n_seq, HEADS // nh),
        in_specs=in_specs,
        out_specs=out_specs,
        out_shape=out_shape,
        scratch_shapes=[pltpu.VMEM((seq_len, w), F32),
                        pltpu.VMEM((seq_len, w), F32),
                        pltpu.VMEM((2, nh, HEAD_DIM, HEAD_DIM), F32),
                        pltpu.VMEM((2, 2, (N_LEVELS + 2) * CHUNK, w), F32),
                        pltpu.VMEM((2, 2, CHUNK, w), F32)],
        input_output_aliases={len(args) - 1: 0},
        compiler_params=pltpu.CompilerParams(dimension_semantics=("parallel", "parallel"),
                                             vmem_limit_bytes=VMEM_LIMIT),
        name="hgrn_ctx" if emit_state else "hgrn_lat",
    )(*args)
    return outs


def _mix_kernel(xp_ref, xs_ref, oa_ref, ga_ref, gb_ref, gta_ref, gtb_ref, gap_ref, gan_ref, gbp_ref, gbn_ref,
                mod_ref, ng_ref, cw_ref, cb_ref, lng_ref, lnb_ref, wco_ref, who_ref, wmo_ref,
                x1_ref, h2_ref, u_scr, conv_scr, *, n_prompt_tiles, tiles_per_sample):
    tm = MIX_TILE
    hl = CONV_HALO
    i = pl.program_id(0)
    j = lax.rem(jnp.maximum(i - n_prompt_tiles, 0), tiles_per_sample)
    is_prompt = i < n_prompt_tiles
    keep_prev = jnp.where(jnp.logical_or(is_prompt, j == 0), 0.0, 1.0)
    keep_next = jnp.where(jnp.logical_or(is_prompt, j == tiles_per_sample - 1), 0.0, 1.0)

    def glu(a_ref, b_ref):
        return a_ref[...].astype(F32) * _sigmoid(b_ref[...].astype(F32))

    u_scr[0:hl, :] = glu(gap_ref, gbp_ref) * keep_prev
    u_scr[hl:hl + tm, :] = glu(ga_ref, gb_ref)
    u_scr[hl + tm:hl + tm + hl, :] = glu(gan_ref, gbn_ref) * keep_next

    off0 = hl - CONV_K // 2
    rb = CONV_ROWS
    span = rb + 8 * (-(-(CONV_K + off0) // 8))
    for cblk in range(D_MODEL // LANES):
        cols = slice(cblk * LANES, (cblk + 1) * LANES)
        wv = cw_ref[:, cols]
        taps = [jnp.broadcast_to(wv[k:k + 1, :], (rb, LANES)) for k in range(CONV_K)]
        bias = jnp.broadcast_to(cb_ref[:, cols], (rb, LANES))
        for r0 in range(0, tm, rb):
            ext = u_scr[r0:r0 + span, cols]
            acc = bias
            for s in range(8):
                us = ext if s == 0 else pltpu.roll(ext, span - s, axis=0)
                for q in range(span // 8):
                    k = 8 * q + s - off0
                    if 0 <= k < CONV_K:
                        acc = acc + taps[k] * us[8 * q:8 * q + rb, :]
            conv_scr[r0:r0 + rb, cols] = acc

    acc = conv_scr[...]
    mu = jnp.mean(acc, axis=-1, keepdims=True)
    cen = acc - mu
    var = jnp.mean(cen * cen, axis=-1, keepdims=True)
    y = cen * lax.rsqrt(var + EPS) * lng_ref[...] + lnb_ref[...]
    u2 = y * _sigmoid(y)
    y_b = _dot(u2.astype(BF16), wco_ref[...])
    y_a = _dot(oa_ref[...], who_ref[...])
    mix = _sigmoid(gta_ref[...].astype(F32)) * y_a + _sigmoid(gtb_ref[...].astype(F32)) * y_b
    branch = mod_ref[0, 2:3, :] * _dot(mix.astype(BF16), wmo_ref[...])

    def residual(x_ref):
        x1 = x_ref[...] + branch
        x1_ref[...] = x1
        h2_ref[...] = _rms_rows(x1, ng_ref[...]) * (1.0 + mod_ref[0, 4:5, :]) + mod_ref[0, 3:4, :]

    @pl.when(is_prompt)
    def _():
        residual(xp_ref)

    @pl.when(jnp.logical_not(is_prompt))
    def _():
        residual(xs_ref)


def _mix(x_ctx, x_lat, oa, rest, mod3, ng1, conv_w, conv_b, ln_g, ln_b, wco, who, wmo,
         sample_len):
    n_prompt_tokens = x_ctx.shape[0]
    t = n_prompt_tokens + x_lat.shape[0]
    tm = MIX_TILE
    hl = CONV_HALO
    r = tm // hl
    n_halo_blocks = t // hl
    col0 = 3
    row = functools.partial(_mod_row, tile_tokens=tm, n_prompt_tokens=n_prompt_tokens,
                            sample_len=sample_len)

    def tile(cb):
        return pl.BlockSpec((tm, D_MODEL), lambda i: (i, cb))

    def prev(cb):
        return pl.BlockSpec((hl, D_MODEL), lambda i: (jnp.maximum(i * r - 1, 0), cb))

    def nxt(cb):
        return pl.BlockSpec((hl, D_MODEL), lambda i: (jnp.minimum((i + 1) * r, n_halo_blocks - 1), cb))

    def full(shape):
        return pl.BlockSpec(shape, lambda i: (0,) * len(shape))

    ctx, lat = _split_token_specs(tm, n_prompt_tokens // tm)
    kern = functools.partial(_mix_kernel, n_prompt_tiles=n_prompt_tokens // tm,
                             tiles_per_sample=sample_len // tm)
    return pl.pallas_call(
        kern,
        grid=(t // tm,),
        in_specs=[ctx, lat, tile(0), tile(col0), tile(col0 + 1), tile(col0 + 2), tile(col0 + 3),
                  prev(col0), nxt(col0), prev(col0 + 1), nxt(col0 + 1),
                  pl.BlockSpec((1, 6, D_MODEL), lambda i: (row(i), 0, 0)),
                  full((1, D_MODEL)), full((CONV_K, D_MODEL)), full((1, D_MODEL)),
                  full((1, D_MODEL)), full((1, D_MODEL)),
                  full((D_MODEL, D_MODEL)), full((D_MODEL, D_MODEL)), full((D_MODEL, D_MODEL))],
        out_specs=[tile(0), tile(0)],
        out_shape=[jax.ShapeDtypeStruct((t, D_MODEL), F32), jax.ShapeDtypeStruct((t, D_MODEL), F32)],
        scratch_shapes=[pltpu.VMEM((tm + 2 * hl, D_MODEL), F32),
                        pltpu.VMEM((tm, D_MODEL), F32)],
        compiler_params=pltpu.CompilerParams(dimension_semantics=("parallel",),
                                             vmem_limit_bytes=VMEM_LIMIT),
        name="mix",
    )(x_ctx, x_lat, oa, rest, rest, rest, rest, rest, rest, rest, rest, mod3, ng1, conv_w, conv_b, ln_g, ln_b,
      wco, who, wmo)


def _cand_rows():
    return [(a, PEER_TOPK // (a + 1)) for a in range(PEER_TOPK)]


N_CAND = sum(nb for _, nb in _cand_rows())
N_CAND_PAD = -(-N_CAND // 8) * 8


def _topk_kernel(h2_ref, wqh_ref, wql_ref, keys_ref, rank_ref, e2_ref, n_ref, b_ref,
                 q_scr, hh_scr, hl_scr, v_scr, cand_scr, s_scr, pairs_scr):
    nk = PEER_NKEYS
    tm = h2_ref.shape[0]
    hh, hl = _split2(h2_ref[...])
    hh_scr[...] = hh
    hl_scr[...] = hl
    neg = -jnp.inf

    def project(h, slot):
        rows = pl.ds(pl.multiple_of(h * 2 * nk, 2 * nk), 2 * nk)
        wqh = wqh_ref[rows, :]
        q_scr[slot] = (_dot_nt(wqh, hh_scr[...]) + _dot_nt(wqh, hl_scr[...])
                       + _dot_nt(wql_ref[rows, :], hh_scr[...]))

    def emit(h, lanes, s1, s2, v1, v2, rank2, cnt, zsum):
        rank_ref[h, :, lanes] = _pack_pairs(rank2)
        n_ref[h, :, lanes] = cnt
        b_ref[h, :, lanes] = jnp.exp(s1 - v1[0:1, :]) * (0.5 / zsum)
        e2_ref[h, :, lanes] = _pack_pairs(jnp.exp(s2 - v2[0:1, :]))

    def pick_first(cur, hit, index, n):
        first = jnp.min(jnp.where(hit, index, float(n)), axis=0, keepdims=True)
        return index == first

    def select_with_ties(h, lanes):
        k = PEER_TOPK
        key_idx = lax.broadcasted_iota(jnp.int32, (nk, LANES), 0).astype(F32)
        ranks = []
        for p in range(2):
            cur = s_scr[h, p, :, lanes]
            rank = jnp.full(cur.shape, float(k), F32)
            for a in range(k):
                m = jnp.max(cur, axis=0, keepdims=True)
                sel = pick_first(cur, cur == m, key_idx, nk)
                rank = jnp.where(sel, float(a), rank)
                cur = jnp.where(sel, neg, cur)
                v_scr[p, a:a + 1, :] = m
            ranks.append(rank)
        rank1, rank2 = ranks
        v1 = v_scr[0]
        v2 = v_scr[1]
        for a in range(k):
            pairs_scr[a * k:(a + 1) * k, :] = v1[a:a + 1, :] + v2
        cand = pairs_scr[...]
        pos = lax.broadcasted_iota(jnp.int32, cand.shape, 0).astype(F32)
        cur = cand
        chosen = jnp.zeros(cand.shape, F32)
        for a in range(k):
            m = jnp.max(cur, axis=0, keepdims=True)
            sel = pick_first(cur, cur == m, pos, k * k)
            chosen = jnp.where(sel, 1.0, chosen)
            cur = jnp.where(sel, neg, cur)
        top = v1[0:1, :] + v2[0:1, :]
        zsum = jnp.sum(chosen * jnp.exp(cand - top), axis=0, keepdims=True)
        cnt = jnp.zeros(rank1.shape, F32)
        for a in range(k):
            n_a = jnp.sum(chosen[a * k:(a + 1) * k, :], axis=0, keepdims=True)
            cnt = jnp.where(rank1 == float(a), n_a, cnt)
        emit(h, lanes, s_scr[h, 0, :, lanes], s_scr[h, 1, :, lanes], v1, v2, rank2, cnt, zsum)

    def select(h, slot, tied):
        kparts = [_split2(keys_ref[h, p]) for p in range(2)]
        for lt in range(tm // LANES):
            lanes = slice(lt * LANES, (lt + 1) * LANES)
            s = []
            for p in range(2):
                kh, kl = kparts[p]
                qh, ql = _split2(q_scr[slot, p * nk:(p + 1) * nk, lanes])
                sp = _dot(kh, qh) + _dot(kh, ql) + _dot(kl, qh)
                s_scr[h, p, :, lanes] = sp
                s.append(sp)
                cur = sp
                rank = jnp.full(sp.shape, float(PEER_TOPK), F32)
                for a in range(PEER_TOPK):
                    m = jnp.max(cur, axis=0, keepdims=True)
                    hit = cur == m
                    if p == 1:
                        rank = jnp.where(hit, float(a), rank)
                    cur = jnp.where(hit, neg, cur)
                    v_scr[p, a:a + 1, :] = m
                removed = jnp.sum(jnp.where(cur == neg, 1.0, 0.0), axis=0, keepdims=True)
                tied = jnp.maximum(tied, jnp.where(removed == float(PEER_TOPK), 0.0, 1.0))
            rank2 = rank
            s1, s2 = s
            v1 = v_scr[0]
            v2 = v_scr[1]
            off = 0
            for a, nb in _cand_rows():
                cand_scr[off:off + nb, :] = v1[a:a + 1, :] + v2[0:nb, :]
                off += nb
            if N_CAND_PAD > N_CAND:
                cand_scr[N_CAND:N_CAND_PAD, :] = jnp.full((N_CAND_PAD - N_CAND, LANES), neg, F32)
            cand = cand_scr[...]
            cur = cand
            tau = None
            for a in range(PEER_TOPK):
                tau = jnp.max(cur, axis=0, keepdims=True)
                if a < PEER_TOPK - 1:
                    cur = jnp.where(cur == tau, neg, cur)
            top = v1[0:1, :] + v2[0:1, :]
            above = cand >= tau
            zsum = jnp.sum(jnp.where(above, jnp.exp(cand - top), 0.0), axis=0, keepdims=True)
            n_above = jnp.sum(jnp.where(above, 1.0, 0.0), axis=0, keepdims=True)
            tied = jnp.maximum(tied, jnp.where(n_above == float(PEER_TOPK), 0.0, 1.0))
            cnt = jnp.zeros(s1.shape, F32)
            for b in range(PEER_TOPK):
                cnt = cnt + jnp.where(s1 + v2[b:b + 1, :] >= tau, 1.0, 0.0)
            emit(h, lanes, s1, s2, v1, v2, rank2, cnt, zsum)
        return tied

    def pair(h0, last, tied):
        for par in range(2):
            h = h0 + par
            if not (last and par == 1):
                project(h + 1, 1 - par)
            tied = select(h, par, tied)
        return tied

    project(0, 0)
    tied = lax.fori_loop(0, PEER_HEADS // 2 - 1, lambda hp, t: pair(2 * hp, False, t),
                         jnp.zeros((1, LANES), F32))
    tied = pair(PEER_HEADS - 2, True, tied)

    @pl.when(jnp.max(tied) > 0.0)
    def _():
        def redo(h, carry):
            for lt in range(tm // LANES):
                select_with_ties(h, slice(lt * LANES, (lt + 1) * LANES))
            return carry

        lax.fori_loop(0, PEER_HEADS, redo, 0)


def _topk(h2, wq_t_hi, wq_t_lo, keys):
    t = h2.shape[0]
    tm = 256
    nq = wq_t_hi.shape[0]
    route = pl.BlockSpec((PEER_HEADS, PEER_NKEYS, tm), lambda i: (0, 0, i))
    packed = pl.BlockSpec((PEER_HEADS, PEER_NKEYS // 2, tm), lambda i: (0, 0, i))

    def route_shape(dt):
        return jax.ShapeDtypeStruct((PEER_HEADS, PEER_NKEYS, t), dt)

    packed_shape = jax.ShapeDtypeStruct((PEER_HEADS, PEER_NKEYS // 2, t), jnp.uint32)

    return pl.pallas_call(
        _topk_kernel,
        grid=(t // tm,),
        in_specs=[pl.BlockSpec((tm, D_MODEL), lambda i: (i, 0)),
                  pl.BlockSpec((nq, D_MODEL), lambda i: (0, 0)),
                  pl.BlockSpec((nq, D_MODEL), lambda i: (0, 0)),
                  pl.BlockSpec(keys.shape, lambda i: (0, 0, 0, 0))],
        out_specs=[packed, packed, route, route],
        out_shape=[packed_shape, packed_shape, route_shape(F32), route_shape(F32)],
        scratch_shapes=[pltpu.VMEM((2, 2 * PEER_NKEYS, tm), F32),
                        pltpu.VMEM((tm, D_MODEL), BF16),
                        pltpu.VMEM((tm, D_MODEL), BF16),
                        pltpu.VMEM((2, PEER_TOPK, LANES), F32),
                        pltpu.VMEM((N_CAND_PAD, LANES), F32),
                        pltpu.VMEM((PEER_HEADS, 2, PEER_NKEYS, tm), F32),
                        pltpu.VMEM((PEER_TOPK * PEER_TOPK, LANES), F32)],
        compiler_params=pltpu.CompilerParams(dimension_semantics=("parallel",),
                                             vmem_limit_bytes=VMEM_LIMIT),
        name="peer_topk",
    )(h2, wq_t_hi, wq_t_lo, keys)


PEER_ROWS_PER_HALF = 8
PEER_ROW_GROUP = 4


def _peer_kernel(h2_ref, x1_ref, mod_ref, fg_ref, rank_ref, e2_ref, n_ref, b_ref, u_ref, vta_ref, vtb_ref,
                 yp_ref, ys_ref, acc_scr, xb_scr, a0_scr, a1_scr, g0_scr, g1_scr, w0_scr, w1_scr, *,
                 n_prompt_tiles):
    nk = PEER_NKEYS
    rph = PEER_ROWS_PER_HALF
    nbh = rph * nk
    tm = h2_ref.shape[0]
    s = pl.program_id(1)

    @pl.when(s == 0)
    def _():
        acc_scr[...] = jnp.zeros(acc_scr.shape, F32)
        xb_scr[...] = h2_ref[...].astype(BF16)
        g1_scr[...] = jnp.zeros(g1_scr.shape, BF16)

    zero = jnp.zeros((nk, LANES), BF16)

    def packed_row(ref, h, ii, lanes):
        return jnp.broadcast_to(ref[h, ii:ii + 1, lanes], (nk, LANES)).astype(BF16)

    def gate_weights(row0, w_scr):
        for lt in range(tm // LANES):
            lanes = slice(lt * LANES, (lt + 1) * LANES)
            for g0 in range(0, rph, PEER_ROW_GROUP):
                w = [zero] * PEER_ROW_GROUP
                for h in range(PEER_HEADS):
                    rk = _unpack_pairs(rank_ref[h, :, lanes])
                    ev = _unpack_pairs(e2_ref[h, :, lanes])
                    for r in range(PEER_ROW_GROUP):
                        cnt = packed_row(n_ref, h, row0 + g0 + r, lanes)
                        coef = packed_row(b_ref, h, row0 + g0 + r, lanes)
                        w[r] = w[r] + jnp.where(rk < cnt, ev * coef, zero)
                for r in range(PEER_ROW_GROUP):
                    w_scr[(g0 + r) * nk:(g0 + r + 1) * nk, lanes] = w[r]

    def activate(a_scr, w_scr, g_scr):
        a = a_scr[...]
        gel = a * (1.0 + lax.erf(a * 0.7071067811865476))
        g_scr[...] = w_scr[...] * gel.astype(BF16)

    xb = xb_scr[...]
    gate_weights(0, w0_scr)
    acc_scr[...] += _dot(vta_ref[...], g1_scr[...])
    a0_scr[...] = _dot_nt(u_ref[0:nbh, :], xb)
    a1_scr[...] = _dot_nt(u_ref[nbh:2 * nbh, :], xb)
    activate(a0_scr, w0_scr, g0_scr)
    gate_weights(rph, w1_scr)
    acc_scr[...] += _dot(vtb_ref[:, 0:nbh], g0_scr[...])
    activate(a1_scr, w1_scr, g1_scr)

    def finish(y_ref):
        pe = acc_scr[...] + _dot(vtb_ref[:, nbh:2 * nbh], g1_scr[...])
        x2 = x1_ref[...] + mod_ref[0, 5:6, :] * pe.T
        y_ref[...] = _rms_rows(x2, fg_ref[...])

    last = s == pl.num_programs(1) - 1
    is_prompt = pl.program_id(0) < n_prompt_tiles

    @pl.when(jnp.logical_and(last, is_prompt))
    def _():
        finish(yp_ref)

    @pl.when(jnp.logical_and(last, jnp.logical_not(is_prompt)))
    def _():
        finish(ys_ref)


def _peer(h2, x1, mod3, final_g, rank2, e2, cnt, bcoef, u_bf, vt_bf, n_prompt_tokens, sample_len):
    t = h2.shape[0]
    tm = 512
    rph = PEER_ROWS_PER_HALF
    nbh = rph * PEER_NKEYS
    n_exp = u_bf.shape[0]
    row = functools.partial(_mod_row, tile_tokens=tm, n_prompt_tokens=n_prompt_tokens,
                            sample_len=sample_len)
    route = pl.BlockSpec((PEER_HEADS, PEER_NKEYS // 2, tm), lambda i, s: (0, 0, i))
    route_rows = pl.BlockSpec((PEER_HEADS, 2 * rph, tm), lambda i, s: (0, s, i))
    tok = pl.BlockSpec((tm, D_MODEL), lambda i, s: (i, 0))
    return pl.pallas_call(
        functools.partial(_peer_kernel, n_prompt_tiles=n_prompt_tokens // tm),
        grid=(t // tm, n_exp // (2 * nbh)),
        in_specs=[tok, tok,
                  pl.BlockSpec((1, 6, D_MODEL), lambda i, s: (row(i), 0, 0)),
                  pl.BlockSpec((1, D_MODEL), lambda i, s: (0, 0)),
                  route, route, route_rows, route_rows,
                  pl.BlockSpec((2 * nbh, D_MODEL), lambda i, s: (s, 0)),
                  pl.BlockSpec((D_MODEL, nbh), lambda i, s: (0, jnp.maximum(2 * s - 1, 0))),
                  pl.BlockSpec((D_MODEL, 2 * nbh), lambda i, s: (0, s))],
        out_specs=list(_split_token_specs(tm, n_prompt_tokens // tm)),
        out_shape=[jax.ShapeDtypeStruct((n_prompt_tokens, D_MODEL), F32),
                   jax.ShapeDtypeStruct((t - n_prompt_tokens, D_MODEL), F32)],
        scratch_shapes=[pltpu.VMEM((D_MODEL, tm), F32),
                        pltpu.VMEM((tm, D_MODEL), BF16),
                        pltpu.VMEM((nbh, tm), F32),
                        pltpu.VMEM((nbh, tm), F32),
                        pltpu.VMEM((nbh, tm), BF16),
                        pltpu.VMEM((nbh, tm), BF16),
                        pltpu.VMEM((nbh, tm), BF16),
                        pltpu.VMEM((nbh, tm), BF16)],
        compiler_params=pltpu.CompilerParams(dimension_semantics=("arbitrary", "arbitrary"),
                                             vmem_limit_bytes=VMEM_LIMIT),
        name="peer_experts",
    )(h2, x1, mod3, final_g, rank2, e2, cnt, bcoef, u_bf, vt_bf, vt_bf)


def kernel(x_prompt, x_sample, state_hgrn, c, c_ctx, norm_g, w_ada, b_ada, w_in, hgrn_lb_logits, hgrn_gnorm_g, w_hgrn_out, conv_w, conv_b, conv_ln_g, conv_ln_b, w_conv_out, w_mix_out, peer_wq, peer_keys, peer_u, peer_v, final_g):
    n_prompt, prompt_len, d = x_prompt.shape
    n_sample, sample_len, _ = x_sample.shape
    depth = w_in.shape[0]
    assert d == D_MODEL and depth == 1 and prompt_len == MIX_TILE
    assert n_sample + 1 <= 8 and sample_len % MIX_TILE == 0
    n_prompt_tokens = n_prompt * prompt_len
    assert n_prompt_tokens % sample_len == 0
    layer = 0
    hw = HEADS * HEAD_DIM

    x_ctx = x_prompt.reshape(n_prompt_tokens, d)
    x_lat = x_sample.reshape(n_sample * sample_len, d)
    n_tokens = n_prompt_tokens + n_sample * sample_len

    cond8 = jnp.zeros((8, d), F32).at[0].set(c_ctx).at[1:1 + n_sample].set(c)
    mod3 = _ada(cond8, w_ada[layer], b_ada[layer]).reshape(8, 6, d)

    w = w_in[layer]
    w_perm = jnp.concatenate([w[:, hw:3 * hw], w[:, :hw], w[:, 3 * hw:]], axis=1).astype(BF16)
    gates, rest = _inproj(x_ctx, x_lat, mod3, norm_g[layer, 0:1], w_perm, sample_len)

    a_np, m_np = _hgrn_consts()
    a_c = jnp.asarray(a_np, BF16)
    m_c = jnp.asarray(m_np, F32)
    gnorm = hgrn_gnorm_g[layer].reshape(1, HEAD_DIM)
    oa0 = jnp.zeros((n_tokens, d), BF16)
    oa1, new_state = _hgrn(gates, rest, 0, n_prompt, prompt_len, hgrn_lb_logits, gnorm, a_c, m_c,
                           None, True, oa0)
    (oa,) = _hgrn(gates, rest, n_prompt_tokens // sample_len, n_sample, sample_len, hgrn_lb_logits,
                  gnorm, a_c, m_c, state_hgrn, False, oa1)

    row = lambda v: v.reshape(1, d)
    x1, h2 = _mix(x_ctx, x_lat, oa, rest, mod3, norm_g[layer, 1:2], conv_w[layer], row(conv_b[layer]),
                  row(conv_ln_g[layer]), row(conv_ln_b[layer]), w_conv_out[layer].astype(BF16),
                  w_hgrn_out[layer].astype(BF16), w_mix_out[layer].astype(BF16), sample_len)

    wq_t = peer_wq[layer].T
    wq_hi = wq_t.astype(BF16)
    wq_lo = (wq_t - wq_hi.astype(F32)).astype(BF16)
    rank2, e2, cnt, bcoef = _topk(h2, wq_hi, wq_lo, peer_keys[layer])

    y_ctx, y_lat = _peer(h2, x1, mod3, row(final_g), rank2, e2, cnt, bcoef, peer_u[layer].astype(BF16),
                         peer_v[layer].T.astype(BF16), n_prompt_tokens, sample_len)
    return (y_ctx.reshape(n_prompt, prompt_len, d), y_lat.reshape(n_sample, sample_len, d),
            new_state)
```
